```python
import math
import jax, jax.numpy as jnp
from jax import lax
import numpy as np

D_MODEL = 2048
BATCH = 4
SEQ = 4096
DEPTH = 1
DEC_BATCH = 2
DEC_SEQ = 16384
PAST_LEN = 128

N_Q_HEADS = 8
N_KV_HEADS = 2
HEAD_DIM = 128
WINDOW = 128
BLOCK = 128
ATTN_WIDTH = N_Q_HEADS * HEAD_DIM
KV_WIDTH = N_KV_HEADS * HEAD_DIM
CONV_WIDTH = 1024
CONV_K = 3
N_MEM = 256
MEM_HEADS = 4
MEM_HEAD_DIM = 256
MEM_WIDTH = MEM_HEADS * MEM_HEAD_DIM
N_BRANCHES = 3
PEER_HEADS = 8
N_KEYS = 128
N_EXPERTS = N_KEYS * N_KEYS
PEER_HALF = 128
PEER_QDIM = 2 * PEER_HALF
PEER_TOPK = 16
PEER_CHUNK = 128
LN_EPS = 1e-5
SPLIT_SIZES = (ATTN_WIDTH, KV_WIDTH, KV_WIDTH, CONV_WIDTH, CONV_WIDTH, CONV_WIDTH, MEM_WIDTH, N_BRANCHES * D_MODEL)
IN_WIDTH = sum(SPLIT_SIZES)

kernel_name = 'hybrid_swa_conv_mem_peer_encoder'


def layer_norm(x, g, b):
    xf = x.astype(jnp.float32)
    mu = xf.mean(-1, keepdims=True)
    var = jnp.square(xf - mu).mean(-1, keepdims=True)
    return ((xf - mu) * lax.rsqrt(var + LN_EPS)).astype(x.dtype) * g + b


def alibi_slopes(n_heads):
    return jnp.exp2(-8.0 * jnp.arange(1, n_heads + 1, dtype=jnp.float32) / n_heads)


def windowed_gqa(q, k, v, sink):
    B, S = q.shape[0], q.shape[1]
    nb = S // BLOCK
    G = N_Q_HEADS // N_KV_HEADS
    qb = q.reshape(B, nb, BLOCK, N_KV_HEADS, G, HEAD_DIM)

    def band(t):
        tp = jnp.pad(t, ((0, 0), (BLOCK, BLOCK), (0, 0), (0, 0)))
        tp = tp.reshape(B, nb + 2, BLOCK, N_KV_HEADS, HEAD_DIM)
        return jnp.concatenate([tp[:, :-2], tp[:, 1:-1], tp[:, 2:]], axis=2)

    kb, vb = band(k), band(v)
    s = jnp.einsum('bnqkgd,bnskd->bnkgqs', qb, kb).astype(jnp.float32) / math.sqrt(HEAD_DIM)
    qpos = jnp.arange(nb)[:, None] * BLOCK + jnp.arange(BLOCK)[None, :]
    kpos = jnp.arange(nb)[:, None] * BLOCK - BLOCK + jnp.arange(3 * BLOCK)[None, :]
    dist = jnp.abs(qpos[:, :, None] - kpos[:, None, :])
    valid = (dist <= WINDOW) & (kpos[:, None, :] >= 0) & (kpos[:, None, :] < S)
    slopes = alibi_slopes(N_Q_HEADS).reshape(N_KV_HEADS, G)
    bias = -slopes[None, :, :, None, None] * dist[:, None, None].astype(jnp.float32)
    s = jnp.where(valid[:, None, None], s + bias, -jnp.inf)
    sk = sink.astype(jnp.float32).reshape(N_KV_HEADS, G)[:, :, None, None]
    m = jnp.maximum(s.max(-1, keepdims=True), sk)
    p = jnp.exp(s - m)
    denom = p.sum(-1, keepdims=True) + jnp.exp(sk - m)
    p = (p / denom).astype(v.dtype)
    o = jnp.einsum('bnkgqs,bnskd->bnqkgd', p, vb)
    return o.reshape(B, S, ATTN_WIDTH)


def short_gated_conv(h, b_gate, c_gate, conv_w):
    S = h.shape[1]
    z = c_gate * h
    zp = jnp.pad(z, ((0, 0), (1, 1), (0, 0)))
    conv = zp[:, :S] * conv_w[0] + zp[:, 1:S + 1] * conv_w[1] + zp[:, 2:] * conv_w[2]
    return b_gate * conv


def memory_cross_attn(q, mk, mv):
    B, S = q.shape[0], q.shape[1]
    s = jnp.einsum('bqhd,bmhd->bhqm', q, mk).astype(jnp.float32) / math.sqrt(MEM_HEAD_DIM)
    p = jax.nn.softmax(s, axis=-1).astype(mv.dtype)
    o = jnp.einsum('bhqm,bmhd->bqhd', p, mv)
    return o.reshape(B, S, MEM_WIDTH)


def peer_ffn(x, w_q, sub_keys, u_tab, v_tab):
    B, S, D = x.shape
    xt = x.reshape(-1, PEER_CHUNK, D)

    def chunk(xc):
        C = xc.shape[0]
        q = (xc @ w_q).reshape(C, PEER_HEADS, 2, PEER_HALF)
        sc = jnp.einsum('chpd,hpkd->chpk', q, sub_keys).astype(jnp.float32)
        top_s, top_i = lax.top_k(sc, PEER_TOPK)
        cand_s = (top_s[:, :, 0, :, None] + top_s[:, :, 1, None, :]).reshape(C, PEER_HEADS, -1)
        cand_i = (top_i[:, :, 0, :, None] * N_KEYS + top_i[:, :, 1, None, :]).reshape(C, PEER_HEADS, -1)
        best_s, pos = lax.top_k(cand_s, PEER_TOPK)
        idx = jnp.take_along_axis(cand_i, pos, axis=-1)
        g = jax.nn.softmax(best_s, axis=-1)
        u = u_tab[idx]
        v = v_tab[idx]
        a = jax.nn.gelu(jnp.einsum('cd,chkd->chk', xc, u), approximate=False)
        w = (g * a.astype(jnp.float32)).astype(x.dtype)
        return jnp.einsum('chk,chkd->cd', w, v)

    y = lax.map(chunk, xt)
    return y.reshape(B, S, D)


def encoder_layer(x, mem, w_in, sink, conv_w, w_mem_k, w_mem_v, w_attn_o, w_conv_out, w_mem_o,
                  w_out, ln1_g, ln1_b, w_peer_q, peer_keys, peer_u, peer_v, ln2_g, ln2_b):
    B, S, D = x.shape
    alpha = (2.0 * DEPTH) ** 0.25
    proj = x @ w_in
    cuts = list(np.cumsum(SPLIT_SIZES)[:-1])
    q, k, v, ch, cb, cc, mq, gl = jnp.split(proj, cuts, axis=-1)
    attn = windowed_gqa(q.reshape(B, S, N_Q_HEADS, HEAD_DIM),
                        k.reshape(B, S, N_KV_HEADS, HEAD_DIM),
                        v.reshape(B, S, N_KV_HEADS, HEAD_DIM), sink) @ w_attn_o
    conv = short_gated_conv(ch, cb, cc, conv_w) @ w_conv_out
    M = mem.shape[1]
    mk = (mem @ w_mem_k).reshape(B, M, MEM_HEADS, MEM_HEAD_DIM)
    mv = (mem @ w_mem_v).reshape(B, M, MEM_HEADS, MEM_HEAD_DIM)
    memo = memory_cross_attn(mq.reshape(B, S, MEM_HEADS, MEM_HEAD_DIM), mk, mv) @ w_mem_o
    gates = jax.nn.sigmoid(gl).reshape(B, S, N_BRANCHES, D)
    merged = gates[:, :, 0] * attn + gates[:, :, 1] * conv + gates[:, :, 2] * memo
    h = layer_norm(alpha * x + merged @ w_out, ln1_g, ln1_b)
    return layer_norm(alpha * h + peer_ffn(h, w_peer_q, peer_keys, peer_u, peer_v), ln2_g, ln2_b)


def setup_inputs(seed: int = 0) -> dict:
    key = jax.random.key(seed)
    ks = jax.random.split(key, 24)
    L = DEPTH
    beta = (8.0 * DEPTH) ** -0.25

    def nrm(k, shape, scale):
        return jax.random.normal(k, shape, jnp.float32) * scale

    return {
        'x_prompt': nrm(ks[0], (BATCH, SEQ, D_MODEL), 1.0),
        'x_sample': nrm(ks[1], (DEC_BATCH, DEC_SEQ, D_MODEL), 1.0),
        'mem_prompt': nrm(ks[2], (BATCH, N_MEM, D_MODEL), 1.0),
        'mem_sample': nrm(ks[3], (DEC_BATCH, N_MEM, D_MODEL), 1.0),
        'w_in': nrm(ks[4], (L, D_MODEL, IN_WIDTH), D_MODEL ** -0.5),
        'sink': nrm(ks[5], (L, N_Q_HEADS), 0.5),
        'conv_w': nrm(ks[6], (L, CONV_K, CONV_WIDTH), CONV_K ** -0.5),
        'w_mem_k': nrm(ks[7], (L, D_MODEL, MEM_WIDTH), D_MODEL ** -0.5),
        'w_mem_v': nrm(ks[8], (L, D_MODEL, MEM_WIDTH), D_MODEL ** -0.5),
        'w_attn_o': nrm(ks[9], (L, ATTN_WIDTH, D_MODEL), beta * ATTN_WIDTH ** -0.5),
        'w_conv_out': nrm(ks[10], (L, CONV_WIDTH, D_MODEL), beta * CONV_WIDTH ** -0.5),
        'w_mem_o': nrm(ks[11], (L, MEM_WIDTH, D_MODEL), beta * MEM_WIDTH ** -0.5),
        'w_out': nrm(ks[12], (L, D_MODEL, D_MODEL), beta * D_MODEL ** -0.5),
        'ln1_g': 1.0 + nrm(ks[13], (L, D_MODEL), 0.02),
        'ln1_b': nrm(ks[14], (L, D_MODEL), 0.02),
        'w_peer_q': nrm(ks[15], (L, D_MODEL, PEER_HEADS * PEER_QDIM), D_MODEL ** -0.5),
        'peer_keys': nrm(ks[16], (L, PEER_HEADS, 2, N_KEYS, PEER_HALF), PEER_HALF ** -0.5),
        'peer_u': nrm(ks[17], (L, N_EXPERTS, D_MODEL), D_MODEL ** -0.5),
        'peer_v': nrm(ks[18], (L, N_EXPERTS, D_MODEL), beta * PEER_HEADS ** -0.5),
        'ln2_g': 1.0 + nrm(ks[19], (L, D_MODEL), 0.02),
        'ln2_b': nrm(ks[20], (L, D_MODEL), 0.02),
    }


def reference(x_prompt, x_sample, mem_prompt, mem_sample, w_in, sink, conv_w, w_mem_k, w_mem_v,
              w_attn_o, w_conv_out, w_mem_o, w_out, ln1_g, ln1_b, w_peer_q, peer_keys, peer_u,
              peer_v, ln2_g, ln2_b):
    y_prompt = x_prompt
    y_sample = x_sample
    for l in range(DEPTH):
        params = (w_in[l], sink[l], conv_w[l], w_mem_k[l], w_mem_v[l], w_attn_o[l], w_conv_out[l],
                  w_mem_o[l], w_out[l], ln1_g[l], ln1_b[l], w_peer_q[l], peer_keys[l], peer_u[l],
                  peer_v[l], ln2_g[l], ln2_b[l])
        y_prompt = encoder_layer(y_prompt, mem_prompt, *params)
        y_sample = encoder_layer(y_sample, mem_sample, *params)
    return (y_prompt, y_sample)
```

```python
import functools
import math

import jax
import jax.numpy as jnp
from jax import lax
from jax.experimental import pallas as pl
from jax.experimental.pallas import tpu as pltpu

D_MODEL = 2048
N_Q_HEADS = 8
N_KV_HEADS = 2
GQA_GROUP = N_Q_HEADS // N_KV_HEADS
HEAD_DIM = 128
WINDOW = 128
ATTN_WIDTH = N_Q_HEADS * HEAD_DIM
KV_WIDTH = N_KV_HEADS * HEAD_DIM
CONV_WIDTH = 1024
N_MEM = 256
MEM_HEADS = 4
MEM_HEAD_DIM = 256
MEM_WIDTH = MEM_HEADS * MEM_HEAD_DIM
PEER_HEADS = 8
N_KEYS = 128
N_EXPERTS = N_KEYS * N_KEYS
PEER_HALF = 128
PEER_TOPK = 16
LN_EPS = 1e-5
DEPTH = 1
ALPHA = (2.0 * DEPTH) ** 0.25
GATE_WIDTH = 3 * D_MODEL
IN_WIDTH = ATTN_WIDTH + 2 * KV_WIDTH + 3 * CONV_WIDTH + MEM_WIDTH + GATE_WIDTH

COL_Q = 0
COL_CH = COL_Q + ATTN_WIDTH
COL_CB = COL_CH + CONV_WIDTH
COL_CC = COL_CB + CONV_WIDTH
COL_GL = COL_CC + CONV_WIDTH
COL_MQ = COL_GL + GATE_WIDTH
COL_K = COL_MQ + MEM_WIDTH
COL_V = COL_K + KV_WIDTH

MXU_DT = jnp.bfloat16
GATE_DT = jnp.bfloat16
VMEM_LIMIT = 56 * 1024 * 1024

PROJ_TM = 1024
PROJ_TN = 512
ATTN_BLOCKS = 4
MIX_T = 256
ROUTE_T = 256
PEER_C = 512
PEER_EB = 512
LN2_T = 256

F32 = jnp.float32


def _cparams(*sem):
    return pltpu.CompilerParams(dimension_semantics=sem, vmem_limit_bytes=VMEM_LIMIT)


def _resident(shape):
    nd = len(shape)
    return pl.BlockSpec(shape, lambda *_: (0,) * nd, pipeline_mode=pl.Buffered(1))


def _layer_norm(x, g, b):
    mu = jnp.mean(x, axis=-1, keepdims=True)
    xc = x - mu
    var = jnp.mean(xc * xc, axis=-1, keepdims=True)
    return xc * lax.rsqrt(var + LN_EPS) * g + b


def _mm_kernel(x_ref, w_ref, o_ref, xb_ref):
    @pl.when(pl.program_id(1) == 0)
    def _():
        xb_ref[...] = x_ref[...].astype(xb_ref.dtype)

    o_ref[...] = jnp.dot(xb_ref[...], w_ref[...], preferred_element_type=F32).astype(o_ref.dtype)


def _matmul(x, w, out_dtype, tm, tn):
    m, k = x.shape
    n = w.shape[1]
    tm, tn = min(tm, m), min(tn, n)
    assert m % tm == 0 and n % tn == 0
    return pl.pallas_call(
        _mm_kernel,
        grid=(m // tm, n // tn),
        in_specs=[pl.BlockSpec((tm, k), lambda i, j: (i, 0)),
                  pl.BlockSpec((k, tn), lambda i, j: (0, j))],
        out_specs=pl.BlockSpec((tm, tn), lambda i, j: (i, j)),
        out_shape=jax.ShapeDtypeStruct((m, n), out_dtype),
        scratch_shapes=[pltpu.VMEM((tm, k), w.dtype)],
        compiler_params=_cparams("parallel", "arbitrary"),
        name="matmul",
    )(x, w)


def _attn_kernel(sink_ref, q_ref, kp_ref, km_ref, kn_ref, vp_ref, vm_ref, vn_ref, o_ref, *, tiles_per_seq):
    t = pl.program_id(0)
    pos = t % tiles_per_seq
    lo = jnp.where(pos == 0, WINDOW, 0)
    hi = jnp.where(pos == tiles_per_seq - 1, 2 * WINDOW, 3 * WINDOW)
    kall = jnp.concatenate([kp_ref[...], km_ref[...], kn_ref[...]], axis=0)
    vall = jnp.concatenate([vp_ref[...], vm_ref[...], vn_ref[...]], axis=0)
    qi = lax.broadcasted_iota(jnp.int32, (WINDOW, 3 * WINDOW), 0)
    si = lax.broadcasted_iota(jnp.int32, (WINDOW, 3 * WINDOW), 1)
    dist = jnp.abs(qi + WINDOW - si)
    band = dist <= WINDOW
    distf = dist.astype(F32)
    scale = 1.0 / math.sqrt(HEAD_DIM)
    n_blocks = q_ref.shape[0] // WINDOW
    for r in range(n_blocks):
        valid = band
        if r == 0:
            valid = valid & (si >= lo)
        if r == n_blocks - 1:
            valid = valid & (si < hi)
        rows = slice(r * WINDOW, (r + 1) * WINDOW)
        for kv in range(N_KV_HEADS):
            cols = slice(kv * HEAD_DIM, (kv + 1) * HEAD_DIM)
            kc = kall[r * WINDOW:(r + 3) * WINDOW, cols]
            vc = vall[r * WINDOW:(r + 3) * WINDOW, cols]
            heads = [kv * GQA_GROUP + g for g in range(GQA_GROUP)]
            q4 = jnp.concatenate([q_ref[rows, h * HEAD_DIM:(h + 1) * HEAD_DIM] for h in heads], axis=0)
            s4 = lax.dot_general(q4, kc, (((1,), (1,)), ((), ())), preferred_element_type=F32) * scale
            ps, inv = [], []
            for g, h in enumerate(heads):
                slope = 2.0 ** (-8.0 * (h + 1) / N_Q_HEADS)
                s = jnp.where(valid, s4[g * WINDOW:(g + 1) * WINDOW] - slope * distf, -jnp.inf)
                sk = sink_ref[h]
                m = jnp.maximum(jnp.max(s, axis=-1, keepdims=True), sk)
                p = jnp.exp(s - m)
                denom = jnp.sum(p, axis=-1, keepdims=True) + jnp.exp(sk - m)
                ps.append(p.astype(vc.dtype))
                inv.append(1.0 / denom)
            o4 = jnp.dot(jnp.concatenate(ps, axis=0), vc, preferred_element_type=F32)
            for g, h in enumerate(heads):
                o_ref[rows, h * HEAD_DIM:(h + 1) * HEAD_DIM] = (
                    o4[g * WINDOW:(g + 1) * WINDOW] * inv[g]).astype(o_ref.dtype)


def _attention(proj, sink, seq_len):
    n = proj.shape[0]
    ta = ATTN_BLOCKS * WINDOW
    assert seq_len % ta == 0 and n % seq_len == 0
    nblk = n // WINDOW
    kcol, vcol = COL_K // KV_WIDTH, COL_V // KV_WIDTH

    def halo(col, side):
        if side < 0:
            return pl.BlockSpec((WINDOW, KV_WIDTH), lambda t: (jnp.maximum(t * ATTN_BLOCKS - 1, 0), col))
        return pl.BlockSpec((WINDOW, KV_WIDTH), lambda t: (jnp.minimum((t + 1) * ATTN_BLOCKS, nblk - 1), col))

    def main(col):
        return pl.BlockSpec((ta, KV_WIDTH), lambda t: (t, col))

    return pl.pallas_call(
        functools.partial(_attn_kernel, tiles_per_seq=seq_len // ta),
        grid=(n // ta,),
        in_specs=[pl.BlockSpec(memory_space=pltpu.SMEM),
                  pl.BlockSpec((ta, ATTN_WIDTH), lambda t: (t, COL_Q // ATTN_WIDTH)),
                  halo(kcol, -1), main(kcol), halo(kcol, 1),
                  halo(vcol, -1), main(vcol), halo(vcol, 1)],
        out_specs=pl.BlockSpec((ta, ATTN_WIDTH), lambda t: (t, 0)),
        out_shape=jax.ShapeDtypeStruct((n, ATTN_WIDTH), proj.dtype),
        compiler_params=_cparams("parallel"),
        name="attention",
    )(sink, proj, proj, proj, proj, proj, proj, proj)


def _mixer_kernel(attn_ref, ch_ref, cb_ref, cc_ref, chp_ref, ccp_ref, chn_ref, ccn_ref,
                  g0_ref, g1_ref, g2_ref, mq_ref, mk_ref, mv_ref, convw_ref,
                  wa_ref, wc_ref, wm_ref, o_ref, *, tiles_per_seq):
    t = pl.program_id(0)
    pos = t % tiles_per_seq
    tt = ch_ref.shape[0]
    z = cc_ref[...].astype(F32) * ch_ref[...].astype(F32)
    zp = ccp_ref[7:8, :].astype(F32) * chp_ref[7:8, :].astype(F32)
    zn = ccn_ref[0:1, :].astype(F32) * chn_ref[0:1, :].astype(F32)
    zp = jnp.where(pos == 0, 0.0, zp)
    zn = jnp.where(pos == tiles_per_seq - 1, 0.0, zn)
    row = lax.broadcasted_iota(jnp.int32, z.shape, 0)
    z_up = jnp.where(row == 0, zp, pltpu.roll(z, 1, 0))
    z_dn = jnp.where(row == tt - 1, zn, pltpu.roll(z, tt - 1, 0))
    conv = z_up * convw_ref[0:1, :] + z * convw_ref[1:2, :] + z_dn * convw_ref[2:3, :]
    cv = (cb_ref[...].astype(F32) * conv).astype(wc_ref.dtype)
    mscale = 1.0 / math.sqrt(MEM_HEAD_DIM)
    mo = []
    for hm in range(MEM_HEADS):
        cols = slice(hm * MEM_HEAD_DIM, (hm + 1) * MEM_HEAD_DIM)
        s = lax.dot_general(mq_ref[:, cols], mk_ref[:, cols], (((1,), (1,)), ((), ())),
                            preferred_element_type=F32) * mscale
        p = jnp.exp(s - jnp.max(s, axis=-1, keepdims=True))
        inv = 1.0 / jnp.sum(p, axis=-1, keepdims=True)
        o = jnp.dot(p.astype(mv_ref.dtype), mv_ref[:, cols], preferred_element_type=F32) * inv
        mo.append(o.astype(wm_ref.dtype))
    memo = jnp.concatenate(mo, axis=-1)
    a_out = jnp.dot(attn_ref[...], wa_ref[...], preferred_element_type=F32)
    c_out = jnp.dot(cv, wc_ref[...], preferred_element_type=F32)
    m_out = jnp.dot(memo, wm_ref[...], preferred_element_type=F32)
    merged = (jax.nn.sigmoid(g0_ref[...].astype(F32)) * a_out
              + jax.nn.sigmoid(g1_ref[...].astype(F32)) * c_out
              + jax.nn.sigmoid(g2_ref[...].astype(F32)) * m_out)
    o_ref[...] = merged.astype(o_ref.dtype)


def _mixer(attn, proj, mem_k, mem_v, conv_w, wa, wc, wm, seq_len):
    n = attn.shape[0]
    tt = MIX_T
    assert seq_len % tt == 0
    tiles_per_seq = seq_len // tt
    n8 = n // 8

    def col(c, width):
        return pl.BlockSpec((tt, width), lambda t: (t, c // width))

    def prev8(c):
        return pl.BlockSpec((8, CONV_WIDTH), lambda t: (jnp.maximum(t * (tt // 8) - 1, 0), c // CONV_WIDTH))

    def next8(c):
        return pl.BlockSpec((8, CONV_WIDTH), lambda t: (jnp.minimum((t + 1) * (tt // 8), n8 - 1), c // CONV_WIDTH))

    def mem_spec():
        return pl.BlockSpec((N_MEM, MEM_WIDTH), lambda t: (t // tiles_per_seq, 0))

    return pl.pallas_call(
        functools.partial(_mixer_kernel, tiles_per_seq=tiles_per_seq),
        grid=(n // tt,),
        in_specs=[pl.BlockSpec((tt, ATTN_WIDTH), lambda t: (t, 0)),
                  col(COL_CH, CONV_WIDTH), col(COL_CB, CONV_WIDTH), col(COL_CC, CONV_WIDTH),
                  prev8(COL_CH), prev8(COL_CC), next8(COL_CH), next8(COL_CC),
                  col(COL_GL, D_MODEL), col(COL_GL + D_MODEL, D_MODEL), col(COL_GL + 2 * D_MODEL, D_MODEL),
                  col(COL_MQ, MEM_WIDTH), mem_spec(), mem_spec(),
                  _resident(conv_w.shape), _resident(wa.shape), _resident(wc.shape), _resident(wm.shape)],
        out_specs=pl.BlockSpec((tt, D_MODEL), lambda t: (t, 0)),
        out_shape=jax.ShapeDtypeStruct((n, D_MODEL), attn.dtype),
        compiler_params=_cparams("parallel"),
        name="mixer",
    )(attn, proj, proj, proj, proj, proj, proj, proj, proj, proj, proj, proj, mem_k, mem_v, conv_w, wa, wc, wm)


def _top16(s):
    rank = jnp.full(s.shape, float(PEER_TOPK), F32)
    vals = []
    for r in range(PEER_TOPK):
        mx = jnp.max(s, axis=0, keepdims=True)
        hit = s == mx
        rank = jnp.where(hit, float(r), rank)
        s = jnp.where(hit, -jnp.inf, s)
        vals.append(mx)
    return rank, jnp.concatenate(vals, axis=0)


def _kth_pair_sum(v0, v1):
    sub = lax.broadcasted_iota(jnp.int32, (8, v0.shape[1]), 0)
    parts = [v0[0:1] + v1, v0[1:2] + v1[0:8]]
    for a in range(2, 8):
        parts.append(jnp.where(sub < PEER_TOPK // (a + 1), v0[a:a + 1] + v1[0:8], -jnp.inf))
    parts.append(v0[8:16] + v1[0:1])
    cand = jnp.concatenate(parts, axis=0)
    mx = None
    for _ in range(PEER_TOPK):
        mx = jnp.max(cand, axis=0, keepdims=True)
        cand = jnp.where(cand == mx, -jnp.inf, cand)
    return mx


def _ln1_route_kernel(x_ref, mg_ref, wo_ref, g_ref, b_ref, wq_ref, keys_ref,
                      h_ref, ht_ref, c0_ref, m0_ref, r1_ref, m1_ref, qt_scr):
    mix = jnp.dot(mg_ref[...], wo_ref[...], preferred_element_type=F32)
    h = _layer_norm(ALPHA * x_ref[...] + mix, g_ref[...], b_ref[...])
    h_ref[...] = h
    ht = h.T.astype(ht_ref.dtype)
    ht_ref[...] = ht
    qt_scr[...] = jnp.dot(wq_ref[...], ht, preferred_element_type=F32).astype(qt_scr.dtype)

    def head(hd, carry):
        base = pl.multiple_of(hd * (2 * PEER_HALF), 2 * PEER_HALF)
        s0 = jnp.dot(keys_ref[2 * hd], qt_scr[pl.ds(base, PEER_HALF), :], preferred_element_type=F32)
        s1 = jnp.dot(keys_ref[2 * hd + 1], qt_scr[pl.ds(base + PEER_HALF, PEER_HALF), :],
                     preferred_element_type=F32)
        rank0, v0 = _top16(s0)
        rank1, v1 = _top16(s1)
        thr = _kth_pair_sum(v0, v1)
        e0 = jnp.exp(v0 - v0[0:1])
        e1 = jnp.exp(v1 - v1[0:1])
        cnt = jnp.zeros_like(v0)
        zrow = jnp.zeros_like(v0)
        for b in range(PEER_TOPK):
            sel = (v0 + v1[b:b + 1]) >= thr
            cnt = cnt + jnp.where(sel, 1.0, 0.0)
            zrow = zrow + jnp.where(sel, e1[b:b + 1], 0.0)
        inv_z = 1.0 / jnp.sum(e0 * zrow, axis=0, keepdims=True)
        c0 = jnp.zeros_like(s0)
        for a in range(PEER_TOPK):
            c0 = jnp.where(rank0 == float(a), cnt[a:a + 1], c0)
        out_rows = pl.ds(pl.multiple_of(hd * N_KEYS, N_KEYS), N_KEYS)
        c0_ref[out_rows, :] = c0
        m0_ref[out_rows, :] = jnp.exp(s0 - v0[0:1]) * inv_z
        r1_ref[out_rows, :] = rank1.astype(r1_ref.dtype)
        m1_ref[out_rows, :] = jnp.exp(s1 - v1[0:1]).astype(m1_ref.dtype)
        return carry

    lax.fori_loop(0, PEER_HEADS, head, 0)


def _ln1_route(x, merged, w_out, ln_g, ln_b, wq_t, keys):
    n = x.shape[0]
    tt = ROUTE_T
    rows = PEER_HEADS * N_KEYS
    tok = lambda t: (t, 0)
    tok_t = lambda t: (0, t)
    return pl.pallas_call(
        _ln1_route_kernel,
        grid=(n // tt,),
        in_specs=[pl.BlockSpec((tt, D_MODEL), tok), pl.BlockSpec((tt, D_MODEL), tok),
                  _resident(w_out.shape), _resident(ln_g.shape), _resident(ln_b.shape),
                  _resident(wq_t.shape), _resident(keys.shape)],
        out_specs=[pl.BlockSpec((tt, D_MODEL), tok), pl.BlockSpec((D_MODEL, tt), tok_t),
                   pl.BlockSpec((rows, tt), tok_t), pl.BlockSpec((rows, tt), tok_t),
                   pl.BlockSpec((rows, tt), tok_t), pl.BlockSpec((rows, tt), tok_t)],
        out_shape=[jax.ShapeDtypeStruct((n, D_MODEL), F32),
                   jax.ShapeDtypeStruct((D_MODEL, n), MXU_DT),
                   jax.ShapeDtypeStruct((rows, n), F32), jax.ShapeDtypeStruct((rows, n), F32),
                   jax.ShapeDtypeStruct((rows, n), GATE_DT), jax.ShapeDtypeStruct((rows, n), GATE_DT)],
        scratch_shapes=[pltpu.VMEM((PEER_HEADS * 2 * PEER_HALF, tt), MXU_DT)],
        compiler_params=_cparams("parallel"),
        name="ln1_route",
    )(x, merged, w_out, ln_g, ln_b, wq_t, keys)


def _peer_kernel(ht_ref, u_ref, vt_ref, c0_ref, m0_ref, r1_ref, m1_ref, o_ref, w_scr):
    e = pl.program_id(1)

    @pl.when(e == 0)
    def _():
        o_ref[...] = jnp.zeros_like(o_ref)

    pre = jnp.dot(u_ref[...], ht_ref[...], preferred_element_type=F32)
    act = (0.5 * pre * (1.0 + lax.erf(pre * math.sqrt(0.5)))).astype(GATE_DT)
    per_step = u_ref.shape[0] // N_KEYS
    for ii in range(per_step):
        i = e * per_step + ii
        gate = None
        for hd in range(PEER_HEADS):
            c0 = c0_ref[pl.ds(hd * N_KEYS + i, 1), :].astype(GATE_DT)
            m0 = m0_ref[pl.ds(hd * N_KEYS + i, 1), :].astype(GATE_DT)
            r1 = r1_ref[hd * N_KEYS:(hd + 1) * N_KEYS, :]
            m1 = m1_ref[hd * N_KEYS:(hd + 1) * N_KEYS, :]
            term = jnp.where(r1 < c0, m1, jnp.zeros_like(m1)) * m0
            gate = term if gate is None else gate + term
        blk = slice(ii * N_KEYS, (ii + 1) * N_KEYS)
        w_scr[blk, :] = (gate * act[blk, :]).astype(w_scr.dtype)
    o_ref[...] += jnp.dot(vt_ref[...], w_scr[...], preferred_element_type=F32)


def _peer_dense(ht, u, vt, c0, m0, r1, m1):
    n = ht.shape[1]
    c, eb = PEER_C, PEER_EB
    rows = PEER_HEADS * N_KEYS
    tok_t = lambda t, e: (0, t)
    return pl.pallas_call(
        _peer_kernel,
        grid=(n // c, N_EXPERTS // eb),
        in_specs=[pl.BlockSpec((D_MODEL, c), tok_t),
                  pl.BlockSpec((eb, D_MODEL), lambda t, e: (e, 0)),
                  pl.BlockSpec((D_MODEL, eb), lambda t, e: (0, e)),
                  pl.BlockSpec((rows, c), tok_t), pl.BlockSpec((rows, c), tok_t),
                  pl.BlockSpec((rows, c), tok_t), pl.BlockSpec((rows, c), tok_t)],
        out_specs=pl.BlockSpec((D_MODEL, c), tok_t),
        out_shape=jax.ShapeDtypeStruct((D_MODEL, n), F32),
        scratch_shapes=[pltpu.VMEM((eb, c), MXU_DT)],
        compiler_params=_cparams("parallel", "arbitrary"),
        name="peer_dense",
    )(ht, u, vt, c0, m0, r1, m1)


def _ln2_kernel(h_ref, pt_ref, g_ref, b_ref, o_ref):
    o_ref[...] = _layer_norm(ALPHA * h_ref[...] + pt_ref[...].T, g_ref[...], b_ref[...])


def _ln2(h, peer_t, ln_g, ln_b):
    n = h.shape[0]
    tt = LN2_T
    return pl.pallas_call(
        _ln2_kernel,
        grid=(n // tt,),
        in_specs=[pl.BlockSpec((tt, D_MODEL), lambda t: (t, 0)),
                  pl.BlockSpec((D_MODEL, tt), lambda t: (0, t)),
                  _resident(ln_g.shape), _resident(ln_b.shape)],
        out_specs=pl.BlockSpec((tt, D_MODEL), lambda t: (t, 0)),
        out_shape=jax.ShapeDtypeStruct((n, D_MODEL), F32),
        compiler_params=_cparams("parallel"),
        name="ln2",
    )(h, peer_t, ln_g, ln_b)


def _prepare(w_in, sink, conv_w, w_mem_k, w_mem_v, w_attn_o, w_conv_out, w_mem_o, w_out,
             ln1_g, ln1_b, w_peer_q, peer_keys, peer_u, peer_v, ln2_g, ln2_b):
    cuts, off = {}, 0
    for name, width in (("q", ATTN_WIDTH), ("k", KV_WIDTH), ("v", KV_WIDTH), ("ch", CONV_WIDTH),
                        ("cb", CONV_WIDTH), ("cc", CONV_WIDTH), ("mq", MEM_WIDTH), ("gl", GATE_WIDTH)):
        cuts[name] = w_in[:, off:off + width]
        off += width
    w_in_perm = jnp.concatenate([cuts[k] for k in ("q", "ch", "cb", "cc", "gl", "mq", "k", "v")], axis=1)
    row = lambda v: v.reshape(1, -1).astype(F32)
    return dict(
        w_in=w_in_perm.astype(MXU_DT), sink=sink.astype(F32), conv_w=conv_w.astype(F32),
        w_mem_k=w_mem_k.astype(MXU_DT), w_mem_v=w_mem_v.astype(MXU_DT),
        wa=w_attn_o.astype(MXU_DT), wc=w_conv_out.astype(MXU_DT), wm=w_mem_o.astype(MXU_DT),
        w_out=w_out.astype(MXU_DT), ln1_g=row(ln1_g), ln1_b=row(ln1_b),
        wq_t=w_peer_q.T.astype(MXU_DT),
        keys=peer_keys.reshape(PEER_HEADS * 2, N_KEYS, PEER_HALF).astype(MXU_DT),
        u=peer_u.astype(MXU_DT), vt=peer_v.T.astype(MXU_DT), ln2_g=row(ln2_g), ln2_b=row(ln2_b))


def _encode_group(x, mem, p):
    b, s, d = x.shape
    x2 = x.reshape(b * s, d)
    mem2 = mem.reshape(b * N_MEM, d)
    proj = _matmul(x2, p["w_in"], MXU_DT, PROJ_TM, PROJ_TN)
    mem_k = _matmul(mem2, p["w_mem_k"], MXU_DT, 512, 512)
    mem_v = _matmul(mem2, p["w_mem_v"], MXU_DT, 512, 512)
    attn = _attention(proj, p["sink"], s)
    merged = _mixer(attn, proj, mem_k, mem_v, p["conv_w"], p["wa"], p["wc"], p["wm"], s)
    h, ht, c0, m0, r1, m1 = _ln1_route(x2, merged, p["w_out"], p["ln1_g"], p["ln1_b"], p["wq_t"], p["keys"])
    peer_t = _peer_dense(ht, p["u"], p["vt"], c0, m0, r1, m1)
    return _ln2(h, peer_t, p["ln2_g"], p["ln2_b"]).reshape(b, s, d)


def kernel(x_prompt, x_sample, mem_prompt, mem_sample, w_in, sink, conv_w, w_mem_k, w_mem_v, w_attn_o,
           w_conv_out, w_mem_o, w_out, ln1_g, ln1_b, w_peer_q, peer_keys, peer_u, peer_v, ln2_g, ln2_b):
    assert w_in.shape[0] == DEPTH == 1
    p = _prepare(w_in[0], sink[0], conv_w[0], w_mem_k[0], w_mem_v[0], w_attn_o[0], w_conv_out[0],
                 w_mem_o[0], w_out[0], ln1_g[0], ln1_b[0], w_peer_q[0], peer_keys[0], peer_u[0],
                 peer_v[0], ln2_g[0], ln2_b[0])
    return (_encode_group(x_prompt, mem_prompt, p), _encode_group(x_sample, mem_sample, p))
```

```python
import functools
import math

import jax
import jax.numpy as jnp
from jax import lax
from jax.experimental import pallas as pl
from jax.experimental.pallas import tpu as pltpu

D_MODEL = 2048
N_Q_HEADS = 8
N_KV_HEADS = 2
GQA_GROUP = N_Q_HEADS // N_KV_HEADS
HEAD_DIM = 128
WINDOW = 128
ATTN_WIDTH = N_Q_HEADS * HEAD_DIM
KV_WIDTH = N_KV_HEADS * HEAD_DIM
CONV_WIDTH = 1024
N_MEM = 256
MEM_HEADS = 4
MEM_HEAD_DIM = 256
MEM_WIDTH = MEM_HEADS * MEM_HEAD_DIM
PEER_HEADS = 8
N_KEYS = 128
N_EXPERTS = N_KEYS * N_KEYS
PEER_HALF = 128
PEER_TOPK = 16
LN_EPS = 1e-5
DEPTH = 1
ALPHA = (2.0 * DEPTH) ** 0.25
GATE_WIDTH = 3 * D_MODEL
IN_WIDTH = ATTN_WIDTH + 2 * KV_WIDTH + 3 * CONV_WIDTH + MEM_WIDTH + GATE_WIDTH

COL_Q = 0
COL_CH = COL_Q + ATTN_WIDTH
COL_CB = COL_CH + CONV_WIDTH
COL_CC = COL_CB + CONV_WIDTH
COL_GL = COL_CC + CONV_WIDTH
COL_MQ = COL_GL + GATE_WIDTH
COL_K = COL_MQ + MEM_WIDTH
COL_V = COL_K + KV_WIDTH

MXU_DT = jnp.bfloat16
GATE_DT = jnp.bfloat16
VMEM_LIMIT = 56 * 1024 * 1024

PROJ_TM = 1024
PROJ_TN = 512
ATTN_BLOCKS = 4
MIX_T = 256
ROUTE_T = 256
PEER_C = 512
PEER_EB = 256
PEER_SUB = 4
LN2_T = 256

F32 = jnp.float32


def _cparams(*sem):
    return pltpu.CompilerParams(dimension_semantics=sem, vmem_limit_bytes=VMEM_LIMIT)


def _resident(shape):
    nd = len(shape)
    return pl.BlockSpec(shape, lambda *_: (0,) * nd, pipeline_mode=pl.Buffered(1))


def _layer_norm(x, g, b):
    mu = jnp.mean(x, axis=-1, keepdims=True)
    xc = x - mu
    var = jnp.mean(xc * xc, axis=-1, keepdims=True)
    return xc * lax.rsqrt(var + LN_EPS) * g + b


def _mm_kernel(x_ref, w_ref, o_ref, xb_ref):
    @pl.when(pl.program_id(1) == 0)
    def _():
        xb_ref[...] = x_ref[...].astype(xb_ref.dtype)

    o_ref[...] = jnp.dot(xb_ref[...], w_ref[...], preferred_element_type=F32).astype(o_ref.dtype)


def _matmul(x, w, out_dtype, tm, tn):
    m, k = x.shape
    n = w.shape[1]
    tm, tn = min(tm, m), min(tn, n)
    assert m % tm == 0 and n % tn == 0
    return pl.pallas_call(
        _mm_kernel,
        grid=(m // tm, n // tn),
        in_specs=[pl.BlockSpec((tm, k), lambda i, j: (i, 0)),
                  pl.BlockSpec((k, tn), lambda i, j: (0, j))],
        out_specs=pl.BlockSpec((tm, tn), lambda i, j: (i, j)),
        out_shape=jax.ShapeDtypeStruct((m, n), out_dtype),
        scratch_shapes=[pltpu.VMEM((tm, k), w.dtype)],
        compiler_params=_cparams("parallel", "arbitrary"),
        name="matmul",
    )(x, w)


def _attn_kernel(sink_ref, q_ref, kp_ref, km_ref, kn_ref, vp_ref, vm_ref, vn_ref, o_ref, *, tiles_per_seq):
    t = pl.program_id(0)
    pos = t % tiles_per_seq
    lo = jnp.where(pos == 0, WINDOW, 0)
    hi = jnp.where(pos == tiles_per_seq - 1, 2 * WINDOW, 3 * WINDOW)
    kall = jnp.concatenate([kp_ref[...], km_ref[...], kn_ref[...]], axis=0)
    vall = jnp.concatenate([vp_ref[...], vm_ref[...], vn_ref[...]], axis=0)
    qi = lax.broadcasted_iota(jnp.int32, (WINDOW, 3 * WINDOW), 0)
    si = lax.broadcasted_iota(jnp.int32, (WINDOW, 3 * WINDOW), 1)
    dist = jnp.abs(qi + WINDOW - si)
    band = dist <= WINDOW
    distf = dist.astype(F32)
    scale = 1.0 / math.sqrt(HEAD_DIM)
    n_blocks = q_ref.shape[0] // WINDOW
    for r in range(n_blocks):
        valid = band
        if r == 0:
            valid = valid & (si >= lo)
        if r == n_blocks - 1:
            valid = valid & (si < hi)
        rows = slice(r * WINDOW, (r + 1) * WINDOW)
        for kv in range(N_KV_HEADS):
            cols = slice(kv * HEAD_DIM, (kv + 1) * HEAD_DIM)
            kc = kall[r * WINDOW:(r + 3) * WINDOW, cols]
            vc = vall[r * WINDOW:(r + 3) * WINDOW, cols]
            heads = [kv * GQA_GROUP + g for g in range(GQA_GROUP)]
            q4 = jnp.concatenate([q_ref[rows, h * HEAD_DIM:(h + 1) * HEAD_DIM] for h in heads], axis=0)
            s4 = lax.dot_general(q4, kc, (((1,), (1,)), ((), ())), preferred_element_type=F32) * scale
            ps, inv = [], []
            for g, h in enumerate(heads):
                slope = 2.0 ** (-8.0 * (h + 1) / N_Q_HEADS)
                s = jnp.where(valid, s4[g * WINDOW:(g + 1) * WINDOW] - slope * distf, -jnp.inf)
                sk = sink_ref[h]
                m = jnp.maximum(jnp.max(s, axis=-1, keepdims=True), sk)
                p = jnp.exp(s - m)
                denom = jnp.sum(p, axis=-1, keepdims=True) + jnp.exp(sk - m)
                ps.append(p.astype(vc.dtype))
                inv.append(1.0 / denom)
            o4 = jnp.dot(jnp.concatenate(ps, axis=0), vc, preferred_element_type=F32)
            for g, h in enumerate(heads):
                o_ref[rows, h * HEAD_DIM:(h + 1) * HEAD_DIM] = (
                    o4[g * WINDOW:(g + 1) * WINDOW] * inv[g]).astype(o_ref.dtype)


def _attention(proj, sink, seq_len):
    n = proj.shape[0]
    ta = ATTN_BLOCKS * WINDOW
    assert seq_len % ta == 0 and n % seq_len == 0
    nblk = n // WINDOW
    kcol, vcol = COL_K // KV_WIDTH, COL_V // KV_WIDTH

    def halo(col, side):
        if side < 0:
            return pl.BlockSpec((WINDOW, KV_WIDTH), lambda t: (jnp.maximum(t * ATTN_BLOCKS - 1, 0), col))
        return pl.BlockSpec((WINDOW, KV_WIDTH), lambda t: (jnp.minimum((t + 1) * ATTN_BLOCKS, nblk - 1), col))

    def main(col):
        return pl.BlockSpec((ta, KV_WIDTH), lambda t: (t, col))

    return pl.pallas_call(
        functools.partial(_attn_kernel, tiles_per_seq=seq_len // ta),
        grid=(n // ta,),
        in_specs=[pl.BlockSpec(memory_space=pltpu.SMEM),
                  pl.BlockSpec((ta, ATTN_WIDTH), lambda t: (t, COL_Q // ATTN_WIDTH)),
                  halo(kcol, -1), main(kcol), halo(kcol, 1),
                  halo(vcol, -1), main(vcol), halo(vcol, 1)],
        out_specs=pl.BlockSpec((ta, ATTN_WIDTH), lambda t: (t, 0)),
        out_shape=jax.ShapeDtypeStruct((n, ATTN_WIDTH), proj.dtype),
        compiler_params=_cparams("parallel"),
        name="attention",
    )(sink, proj, proj, proj, proj, proj, proj, proj)


def _mixer_kernel(attn_ref, ch_ref, cb_ref, cc_ref, chp_ref, ccp_ref, chn_ref, ccn_ref,
                  g0_ref, g1_ref, g2_ref, mq_ref, mk_ref, mv_ref, convw_ref,
                  wa_ref, wc_ref, wm_ref, o_ref, *, tiles_per_seq):
    t = pl.program_id(0)
    pos = t % tiles_per_seq
    tt = ch_ref.shape[0]
    z = cc_ref[...].astype(F32) * ch_ref[...].astype(F32)
    zp = ccp_ref[7:8, :].astype(F32) * chp_ref[7:8, :].astype(F32)
    zn = ccn_ref[0:1, :].astype(F32) * chn_ref[0:1, :].astype(F32)
    zp = jnp.where(pos == 0, 0.0, zp)
    zn = jnp.where(pos == tiles_per_seq - 1, 0.0, zn)
    row = lax.broadcasted_iota(jnp.int32, z.shape, 0)
    z_up = jnp.where(row == 0, zp, pltpu.roll(z, 1, 0))
    z_dn = jnp.where(row == tt - 1, zn, pltpu.roll(z, tt - 1, 0))
    conv = z_up * convw_ref[0:1, :] + z * convw_ref[1:2, :] + z_dn * convw_ref[2:3, :]
    cv = (cb_ref[...].astype(F32) * conv).astype(wc_ref.dtype)
    mscale = 1.0 / math.sqrt(MEM_HEAD_DIM)
    mo = []
    for hm in range(MEM_HEADS):
        cols = slice(hm * MEM_HEAD_DIM, (hm + 1) * MEM_HEAD_DIM)
        s = lax.dot_general(mq_ref[:, cols], mk_ref[:, cols], (((1,), (1,)), ((), ())),
                            preferred_element_type=F32) * mscale
        p = jnp.exp(s - jnp.max(s, axis=-1, keepdims=True))
        inv = 1.0 / jnp.sum(p, axis=-1, keepdims=True)
        o = jnp.dot(p.astype(mv_ref.dtype), mv_ref[:, cols], preferred_element_type=F32) * inv
        mo.append(o.astype(wm_ref.dtype))
    memo = jnp.concatenate(mo, axis=-1)
    a_out = jnp.dot(attn_ref[...], wa_ref[...], preferred_element_type=F32)
    c_out = jnp.dot(cv, wc_ref[...], preferred_element_type=F32)
    m_out = jnp.dot(memo, wm_ref[...], preferred_element_type=F32)
    merged = (jax.nn.sigmoid(g0_ref[...].astype(F32)) * a_out
              + jax.nn.sigmoid(g1_ref[...].astype(F32)) * c_out
              + jax.nn.sigmoid(g2_ref[...].astype(F32)) * m_out)
    o_ref[...] = merged.astype(o_ref.dtype)


def _mixer(attn, proj, mem_k, mem_v, conv_w, wa, wc, wm, seq_len):
    n = attn.shape[0]
    tt = MIX_T
    assert seq_len % tt == 0
    tiles_per_seq = seq_len // tt
    n8 = n // 8

    def col(c, width):
        return pl.BlockSpec((tt, width), lambda t: (t, c // width))

    def prev8(c):
        return pl.BlockSpec((8, CONV_WIDTH), lambda t: (jnp.maximum(t * (tt // 8) - 1, 0), c // CONV_WIDTH))

    def next8(c):
        return pl.BlockSpec((8, CONV_WIDTH), lambda t: (jnp.minimum((t + 1) * (tt // 8), n8 - 1), c // CONV_WIDTH))

    def mem_spec():
        return pl.BlockSpec((N_MEM, MEM_WIDTH), lambda t: (t // tiles_per_seq, 0))

    return pl.pallas_call(
        functools.partial(_mixer_kernel, tiles_per_seq=tiles_per_seq),
        grid=(n // tt,),
        in_specs=[pl.BlockSpec((tt, ATTN_WIDTH), lambda t: (t, 0)),
                  col(COL_CH, CONV_WIDTH), col(COL_CB, CONV_WIDTH), col(COL_CC, CONV_WIDTH),
                  prev8(COL_CH), prev8(COL_CC), next8(COL_CH), next8(COL_CC),
                  col(COL_GL, D_MODEL), col(COL_GL + D_MODEL, D_MODEL), col(COL_GL + 2 * D_MODEL, D_MODEL),
                  col(COL_MQ, MEM_WIDTH), mem_spec(), mem_spec(),
                  _resident(conv_w.shape), _resident(wa.shape), _resident(wc.shape), _resident(wm.shape)],
        out_specs=pl.BlockSpec((tt, D_MODEL), lambda t: (t, 0)),
        out_shape=jax.ShapeDtypeStruct((n, D_MODEL), attn.dtype),
        compiler_params=_cparams("parallel"),
        name="mixer",
    )(attn, proj, proj, proj, proj, proj, proj, proj, proj, proj, proj, proj, mem_k, mem_v, conv_w, wa, wc, wm)


def _top16(s):
    rank = jnp.full(s.shape, float(PEER_TOPK), F32)
    vals = []
    for r in range(PEER_TOPK):
        mx = jnp.max(s, axis=0, keepdims=True)
        hit = s == mx
        rank = jnp.where(hit, float(r), rank)
        s = jnp.where(hit, -jnp.inf, s)
        vals.append(mx)
    return rank, jnp.concatenate(vals, axis=0)


def _kth_pair_sum(v0, v1):
    sub = lax.broadcasted_iota(jnp.int32, (8, v0.shape[1]), 0)
    parts = [v0[0:1] + v1, v0[1:2] + v1[0:8]]
    for a in range(2, 8):
        parts.append(jnp.where(sub < PEER_TOPK // (a + 1), v0[a:a + 1] + v1[0:8], -jnp.inf))
    parts.append(v0[8:16] + v1[0:1])
    cand = jnp.concatenate(parts, axis=0)
    mx = None
    for _ in range(PEER_TOPK):
        mx = jnp.max(cand, axis=0, keepdims=True)
        cand = jnp.where(cand == mx, -jnp.inf, cand)
    return mx


def _ln1_route_kernel(x_ref, mg_ref, wo_ref, g_ref, b_ref, wq_ref, keys_ref,
                      h_ref, ht_ref, c0_ref, m0_ref, r1_ref, m1_ref, qt_scr):
    mix = jnp.dot(mg_ref[...], wo_ref[...], preferred_element_type=F32)
    h = _layer_norm(ALPHA * x_ref[...] + mix, g_ref[...], b_ref[...])
    h_ref[...] = h
    ht = h.T.astype(ht_ref.dtype)
    ht_ref[...] = ht
    qt_scr[...] = jnp.dot(wq_ref[...], ht, preferred_element_type=F32).astype(qt_scr.dtype)

    def head(hd, carry):
        base = pl.multiple_of(hd * (2 * PEER_HALF), 2 * PEER_HALF)
        s0 = jnp.dot(keys_ref[2 * hd], qt_scr[pl.ds(base, PEER_HALF), :], preferred_element_type=F32)
        s1 = jnp.dot(keys_ref[2 * hd + 1], qt_scr[pl.ds(base + PEER_HALF, PEER_HALF), :],
                     preferred_element_type=F32)
        rank0, v0 = _top16(s0)
        rank1, v1 = _top16(s1)
        thr = _kth_pair_sum(v0, v1)
        e0 = jnp.exp(v0 - v0[0:1])
        e1 = jnp.exp(v1 - v1[0:1])
        cnt = jnp.zeros_like(v0)
        zrow = jnp.zeros_like(v0)
        for b in range(PEER_TOPK):
            sel = (v0 + v1[b:b + 1]) >= thr
            cnt = cnt + jnp.where(sel, 1.0, 0.0)
            zrow = zrow + jnp.where(sel, e1[b:b + 1], 0.0)
        inv_z = 1.0 / jnp.sum(e0 * zrow, axis=0, keepdims=True)
        c0 = jnp.zeros_like(s0)
        for a in range(PEER_TOPK):
            c0 = jnp.where(rank0 == float(a), cnt[a:a + 1], c0)
        out_rows = pl.ds(pl.multiple_of(hd * N_KEYS, N_KEYS), N_KEYS)
        c0_ref[out_rows, :] = c0
        m0_ref[out_rows, :] = jnp.exp(s0 - v0[0:1]) * inv_z
        r1_ref[out_rows, :] = rank1.astype(r1_ref.dtype)
        m1_ref[out_rows, :] = jnp.exp(s1 - v1[0:1]).astype(m1_ref.dtype)
        return carry

    lax.fori_loop(0, PEER_HEADS, head, 0)


def _ln1_route(x, merged, w_out, ln_g, ln_b, wq_t, keys):
    n = x.shape[0]
    tt = ROUTE_T
    rows = PEER_HEADS * N_KEYS
    tok = lambda t: (t, 0)
    tok_t = lambda t: (0, t)
    return pl.pallas_call(
        _ln1_route_kernel,
        grid=(n // tt,),
        in_specs=[pl.BlockSpec((tt, D_MODEL), tok), pl.BlockSpec((tt, D_MODEL), tok),
                  _resident(w_out.shape), _resident(ln_g.shape), _resident(ln_b.shape),
                  _resident(wq_t.shape), _resident(keys.shape)],
        out_specs=[pl.BlockSpec((tt, D_MODEL), tok), pl.BlockSpec((D_MODEL, tt), tok_t),
                   pl.BlockSpec((rows, tt), tok_t), pl.BlockSpec((rows, tt), tok_t),
                   pl.BlockSpec((rows, tt), tok_t), pl.BlockSpec((rows, tt), tok_t)],
        out_shape=[jax.ShapeDtypeStruct((n, D_MODEL), F32),
                   jax.ShapeDtypeStruct((D_MODEL, n), MXU_DT),
                   jax.ShapeDtypeStruct((rows, n), F32), jax.ShapeDtypeStruct((rows, n), F32),
                   jax.ShapeDtypeStruct((rows, n), GATE_DT), jax.ShapeDtypeStruct((rows, n), GATE_DT)],
        scratch_shapes=[pltpu.VMEM((PEER_HEADS * 2 * PEER_HALF, tt), MXU_DT)],
        compiler_params=_cparams("parallel"),
        name="ln1_route",
    )(x, merged, w_out, ln_g, ln_b, wq_t, keys)


def _peer_gate_block(block, pre, c0_ref, m0_ref, r1_ref, m1_ref):
    act = (0.5 * pre * (1.0 + lax.erf(pre * math.sqrt(0.5)))).astype(GATE_DT)
    per_block = pre.shape[0] // N_KEYS
    out = []
    for ii in range(per_block):
        i = block * per_block + ii
        gate = None
        for hd in range(PEER_HEADS):
            c0 = c0_ref[pl.ds(hd * N_KEYS + i, 1), :].astype(GATE_DT)
            m0 = m0_ref[pl.ds(hd * N_KEYS + i, 1), :].astype(GATE_DT)
            r1 = r1_ref[hd * N_KEYS:(hd + 1) * N_KEYS, :]
            m1 = m1_ref[hd * N_KEYS:(hd + 1) * N_KEYS, :]
            term = jnp.where(r1 < c0, m1, jnp.zeros_like(m1)) * m0
            gate = term if gate is None else gate + term
        out.append((gate * act[ii * N_KEYS:(ii + 1) * N_KEYS, :]).astype(MXU_DT))
    return jnp.concatenate(out, axis=0)


def _peer_kernel(ht_ref, u_ref, vt_ref, c0_ref, m0_ref, r1_ref, m1_ref, o_ref):
    g = pl.program_id(1)
    eb = PEER_EB
    n_sub = u_ref.shape[0] // eb

    @pl.when(g == 0)
    def _():
        o_ref[...] = jnp.zeros_like(o_ref)

    routing = (c0_ref, m0_ref, r1_ref, m1_ref)
    pre = [jnp.dot(u_ref[k * eb:(k + 1) * eb, :], ht_ref[...], preferred_element_type=F32)
           for k in range(n_sub)]
    for k in range(n_sub):
        w = _peer_gate_block(g * n_sub + k, pre[k], *routing)
        o_ref[...] += jnp.dot(vt_ref[:, k * eb:(k + 1) * eb], w, preferred_element_type=F32)


def _peer_dense(ht, u, vt, c0, m0, r1, m1):
    n = ht.shape[1]
    c, step = PEER_C, PEER_EB * PEER_SUB
    rows = PEER_HEADS * N_KEYS
    tok_t = lambda t, g: (0, t)
    return pl.pallas_call(
        _peer_kernel,
        grid=(n // c, N_EXPERTS // step),
        in_specs=[pl.BlockSpec((D_MODEL, c), tok_t),
                  pl.BlockSpec((step, D_MODEL), lambda t, g: (g, 0)),
                  pl.BlockSpec((D_MODEL, step), lambda t, g: (0, g)),
                  pl.BlockSpec((rows, c), tok_t), pl.BlockSpec((rows, c), tok_t),
                  pl.BlockSpec((rows, c), tok_t), pl.BlockSpec((rows, c), tok_t)],
        out_specs=pl.BlockSpec((D_MODEL, c), tok_t),
        out_shape=jax.ShapeDtypeStruct((D_MODEL, n), F32),
        compiler_params=_cparams("parallel", "arbitrary"),
        name="peer_dense",
    )(ht, u, vt, c0, m0, r1, m1)


def _ln2_kernel(h_ref, pt_ref, g_ref, b_ref, o_ref):
    o_ref[...] = _layer_norm(ALPHA * h_ref[...] + pt_ref[...].T, g_ref[...], b_ref[...])


def _ln2(h, peer_t, ln_g, ln_b):
    n = h.shape[0]
    tt = LN2_T
    return pl.pallas_call(
        _ln2_kernel,
        grid=(n // tt,),
        in_specs=[pl.BlockSpec((tt, D_MODEL), lambda t: (t, 0)),
                  pl.BlockSpec((D_MODEL, tt), lambda t: (0, t)),
                  _resident(ln_g.shape), _resident(ln_b.shape)],
        out_specs=pl.BlockSpec((tt, D_MODEL), lambda t: (t, 0)),
        out_shape=jax.ShapeDtypeStruct((n, D_MODEL), F32),
        compiler_params=_cparams("parallel"),
        name="ln2",
    )(h, peer_t, ln_g, ln_b)


def _prepare(w_in, sink, conv_w, w_mem_k, w_mem_v, w_attn_o, w_conv_out, w_mem_o, w_out,
             ln1_g, ln1_b, w_peer_q, peer_keys, peer_u, peer_v, ln2_g, ln2_b):
    cuts, off = {}, 0
    for name, width in (("q", ATTN_WIDTH), ("k", KV_WIDTH), ("v", KV_WIDTH), ("ch", CONV_WIDTH),
                        ("cb", CONV_WIDTH), ("cc", CONV_WIDTH), ("mq", MEM_WIDTH), ("gl", GATE_WIDTH)):
        cuts[name] = w_in[:, off:off + width]
        off += width
    w_in_perm = jnp.concatenate([cuts[k] for k in ("q", "ch", "cb", "cc", "gl", "mq", "k", "v")], axis=1)
    row = lambda v: v.reshape(1, -1).astype(F32)
    return dict(
        w_in=w_in_perm.astype(MXU_DT), sink=sink.astype(F32), conv_w=conv_w.astype(F32),
        w_mem_k=w_mem_k.astype(MXU_DT), w_mem_v=w_mem_v.astype(MXU_DT),
        wa=w_attn_o.astype(MXU_DT), wc=w_conv_out.astype(MXU_DT), wm=w_mem_o.astype(MXU_DT),
        w_out=w_out.astype(MXU_DT), ln1_g=row(ln1_g), ln1_b=row(ln1_b),
        wq_t=w_peer_q.T.astype(MXU_DT),
        keys=peer_keys.reshape(PEER_HEADS * 2, N_KEYS, PEER_HALF).astype(MXU_DT),
        u=peer_u.astype(MXU_DT), vt=peer_v.T.astype(MXU_DT), ln2_g=row(ln2_g), ln2_b=row(ln2_b))


def _encode_group(x, mem, p):
    b, s, d = x.shape
    x2 = x.reshape(b * s, d)
    mem2 = mem.reshape(b * N_MEM, d)
    proj = _matmul(x2, p["w_in"], MXU_DT, PROJ_TM, PROJ_TN)
    mem_k = _matmul(mem2, p["w_mem_k"], MXU_DT, 512, 512)
    mem_v = _matmul(mem2, p["w_mem_v"], MXU_DT, 512, 512)
    attn = _attention(proj, p["sink"], s)
    merged = _mixer(attn, proj, mem_k, mem_v, p["conv_w"], p["wa"], p["wc"], p["wm"], s)
    h, ht, c0, m0, r1, m1 = _ln1_route(x2, merged, p["w_out"], p["ln1_g"], p["ln1_b"], p["wq_t"], p["keys"])
    peer_t = _peer_dense(ht, p["u"], p["vt"], c0, m0, r1, m1)
    return _ln2(h, peer_t, p["ln2_g"], p["ln2_b"]).reshape(b, s, d)


def kernel(x_prompt, x_sample, mem_prompt, mem_sample, w_in, sink, conv_w, w_mem_k, w_mem_v, w_attn_o,
           w_conv_out, w_mem_o, w_out, ln1_g, ln1_b, w_peer_q, peer_keys, peer_u, peer_v, ln2_g, ln2_b):
    assert w_in.shape[0] == DEPTH == 1
    p = _prepare(w_in[0], sink[0], conv_w[0], w_mem_k[0], w_mem_v[0], w_attn_o[0], w_conv_out[0],
                 w_mem_o[0], w_out[0], ln1_g[0], ln1_b[0], w_peer_q[0], peer_keys[0], peer_u[0],
                 peer_v[0], ln2_g[0], ln2_b[0])
    return (_encode_group(x_prompt, mem_prompt, p), _encode_group(x_sample, mem_sample, p))
```

```python
import functools
import math

import jax
import jax.numpy as jnp
from jax import lax
from jax.experimental import pallas as pl
from jax.experimental.pallas import tpu as pltpu

D_MODEL = 2048
N_Q_HEADS = 8
N_KV_HEADS = 2
GQA_GROUP = N_Q_HEADS // N_KV_HEADS
HEAD_DIM = 128
WINDOW = 128
ATTN_WIDTH = N_Q_HEADS * HEAD_DIM
KV_WIDTH = N_KV_HEADS * HEAD_DIM
CONV_WIDTH = 1024
N_MEM = 256
MEM_HEADS = 4
MEM_HEAD_DIM = 256
MEM_WIDTH = MEM_HEADS * MEM_HEAD_DIM
PEER_HEADS = 8
N_KEYS = 128
N_EXPERTS = N_KEYS * N_KEYS
PEER_HALF = 128
PEER_TOPK = 16
LN_EPS = 1e-5
DEPTH = 1
ALPHA = (2.0 * DEPTH) ** 0.25
GATE_WIDTH = 3 * D_MODEL
IN_WIDTH = ATTN_WIDTH + 2 * KV_WIDTH + 3 * CONV_WIDTH + MEM_WIDTH + GATE_WIDTH

COL_Q = 0
COL_CH = COL_Q + ATTN_WIDTH
COL_CB = COL_CH + CONV_WIDTH
COL_CC = COL_CB + CONV_WIDTH
COL_GL = COL_CC + CONV_WIDTH
COL_MQ = COL_GL + GATE_WIDTH
COL_K = COL_MQ + MEM_WIDTH
COL_V = COL_K + KV_WIDTH

MXU_DT = jnp.bfloat16
GATE_DT = jnp.bfloat16
VMEM_LIMIT = 56 * 1024 * 1024

PROJ_TM = 1024
PROJ_TN = 512
ATTN_BLOCKS = 4
MIX_T = 256
ROUTE_T = 256
ROUTE_LANES = 128
PEER_C = 512
PEER_EB = 256
PEER_SUB = 8
LN2_T = 256

F32 = jnp.float32


def _cparams(*sem):
    return pltpu.CompilerParams(dimension_semantics=sem, vmem_limit_bytes=VMEM_LIMIT)


def _resident(shape):
    nd = len(shape)
    return pl.BlockSpec(shape, lambda *_: (0,) * nd, pipeline_mode=pl.Buffered(1))


def _layer_norm(x, g, b):
    mu = jnp.mean(x, axis=-1, keepdims=True)
    xc = x - mu
    var = jnp.mean(xc * xc, axis=-1, keepdims=True)
    return xc * lax.rsqrt(var + LN_EPS) * g + b


def _mm_kernel(x_ref, w_ref, o_ref, xb_ref):
    @pl.when(pl.program_id(1) == 0)
    def _():
        xb_ref[...] = x_ref[...].astype(xb_ref.dtype)

    o_ref[...] = jnp.dot(xb_ref[...], w_ref[...], preferred_element_type=F32).astype(o_ref.dtype)


def _matmul(x, w, out_dtype, tm, tn):
    m, k = x.shape
    n = w.shape[1]
    tm, tn = min(tm, m), min(tn, n)
    assert m % tm == 0 and n % tn == 0
    return pl.pallas_call(
        _mm_kernel,
        grid=(m // tm, n // tn),
        in_specs=[pl.BlockSpec((tm, k), lambda i, j: (i, 0)),
                  pl.BlockSpec((k, tn), lambda i, j: (0, j))],
        out_specs=pl.BlockSpec((tm, tn), lambda i, j: (i, j)),
        out_shape=jax.ShapeDtypeStruct((m, n), out_dtype),
        scratch_shapes=[pltpu.VMEM((tm, k), w.dtype)],
        compiler_params=_cparams("parallel", "arbitrary"),
        name="matmul",
    )(x, w)


def _attn_kernel(sink_ref, q_ref, kp_ref, km_ref, kn_ref, vp_ref, vm_ref, vn_ref, o_ref, *, tiles_per_seq):
    t = pl.program_id(0)
    pos = t % tiles_per_seq
    lo = jnp.where(pos == 0, WINDOW, 0)
    hi = jnp.where(pos == tiles_per_seq - 1, 2 * WINDOW, 3 * WINDOW)
    kall = jnp.concatenate([kp_ref[...], km_ref[...], kn_ref[...]], axis=0)
    vall = jnp.concatenate([vp_ref[...], vm_ref[...], vn_ref[...]], axis=0)
    qi = lax.broadcasted_iota(jnp.int32, (WINDOW, 3 * WINDOW), 0)
    si = lax.broadcasted_iota(jnp.int32, (WINDOW, 3 * WINDOW), 1)
    dist = jnp.abs(qi + WINDOW - si)
    band = dist <= WINDOW
    distf = dist.astype(F32)
    scale = 1.0 / math.sqrt(HEAD_DIM)
    n_blocks = q_ref.shape[0] // WINDOW
    for r in range(n_blocks):
        valid = band
        if r == 0:
            valid = valid & (si >= lo)
        if r == n_blocks - 1:
            valid = valid & (si < hi)
        rows = slice(r * WINDOW, (r + 1) * WINDOW)
        for kv in range(N_KV_HEADS):
            cols = slice(kv * HEAD_DIM, (kv + 1) * HEAD_DIM)
            kc = kall[r * WINDOW:(r + 3) * WINDOW, cols]
            vc = vall[r * WINDOW:(r + 3) * WINDOW, cols]
            heads = [kv * GQA_GROUP + g for g in range(GQA_GROUP)]
            q4 = jnp.concatenate([q_ref[rows, h * HEAD_DIM:(h + 1) * HEAD_DIM] for h in heads], axis=0)
            s4 = lax.dot_general(q4, kc, (((1,), (1,)), ((), ())), preferred_element_type=F32) * scale
            ps, inv = [], []
            for g, h in enumerate(heads):
                slope = 2.0 ** (-8.0 * (h + 1) / N_Q_HEADS)
                s = jnp.where(valid, s4[g * WINDOW:(g + 1) * WINDOW] - slope * distf, -jnp.inf)
                sk = sink_ref[h]
                m = jnp.maximum(jnp.max(s, axis=-1, keepdims=True), sk)
                p = jnp.exp(s - m)
                denom = jnp.sum(p, axis=-1, keepdims=True) + jnp.exp(sk - m)
                ps.append(p.astype(vc.dtype))
                inv.append(1.0 / denom)
            o4 = jnp.dot(jnp.concatenate(ps, axis=0), vc, preferred_element_type=F32)
            for g, h in enumerate(heads):
                o_ref[rows, h * HEAD_DIM:(h + 1) * HEAD_DIM] = (
                    o4[g * WINDOW:(g + 1) * WINDOW] * inv[g]).astype(o_ref.dtype)


def _attention(proj, sink, seq_len):
    n = proj.shape[0]
    ta = ATTN_BLOCKS * WINDOW
    assert seq_len % ta == 0 and n % seq_len == 0
    nblk = n // WINDOW
    kcol, vcol = COL_K // KV_WIDTH, COL_V // KV_WIDTH

    def halo(col, side):
        if side < 0:
            return pl.BlockSpec((WINDOW, KV_WIDTH), lambda t: (jnp.maximum(t * ATTN_BLOCKS - 1, 0), col))
        return pl.BlockSpec((WINDOW, KV_WIDTH), lambda t: (jnp.minimum((t + 1) * ATTN_BLOCKS, nblk - 1), col))

    def main(col):
        return pl.BlockSpec((ta, KV_WIDTH), lambda t: (t, col))

    return pl.pallas_call(
        functools.partial(_attn_kernel, tiles_per_seq=seq_len // ta),
        grid=(n // ta,),
        in_specs=[pl.BlockSpec(memory_space=pltpu.SMEM),
                  pl.BlockSpec((ta, ATTN_WIDTH), lambda t: (t, COL_Q // ATTN_WIDTH)),
                  halo(kcol, -1), main(kcol), halo(kcol, 1),
                  halo(vcol, -1), main(vcol), halo(vcol, 1)],
        out_specs=pl.BlockSpec((ta, ATTN_WIDTH), lambda t: (t, 0)),
        out_shape=jax.ShapeDtypeStruct((n, ATTN_WIDTH), proj.dtype),
        compiler_params=_cparams("parallel"),
        name="attention",
    )(sink, proj, proj, proj, proj, proj, proj, proj)


def _mixer_kernel(attn_ref, ch_ref, cb_ref, cc_ref, chp_ref, ccp_ref, chn_ref, ccn_ref,
                  g0_ref, g1_ref, g2_ref, mq_ref, mk_ref, mv_ref, convw_ref,
                  wa_ref, wc_ref, wm_ref, o_ref, *, tiles_per_seq):
    t = pl.program_id(0)
    pos = t % tiles_per_seq
    tt = ch_ref.shape[0]
    z = cc_ref[...].astype(F32) * ch_ref[...].astype(F32)
    zp = ccp_ref[7:8, :].astype(F32) * chp_ref[7:8, :].astype(F32)
    zn = ccn_ref[0:1, :].astype(F32) * chn_ref[0:1, :].astype(F32)
    zp = jnp.where(pos == 0, 0.0, zp)
    zn = jnp.where(pos == tiles_per_seq - 1, 0.0, zn)
    row = lax.broadcasted_iota(jnp.int32, z.shape, 0)
    z_up = jnp.where(row == 0, zp, pltpu.roll(z, 1, 0))
    z_dn = jnp.where(row == tt - 1, zn, pltpu.roll(z, tt - 1, 0))
    conv = z_up * convw_ref[0:1, :] + z * convw_ref[1:2, :] + z_dn * convw_ref[2:3, :]
    cv = (cb_ref[...].astype(F32) * conv).astype(wc_ref.dtype)
    mscale = 1.0 / math.sqrt(MEM_HEAD_DIM)
    mo = []
    for hm in range(MEM_HEADS):
        cols = slice(hm * MEM_HEAD_DIM, (hm + 1) * MEM_HEAD_DIM)
        s = lax.dot_general(mq_ref[:, cols], mk_ref[:, cols], (((1,), (1,)), ((), ())),
                            preferred_element_type=F32) * mscale
        p = jnp.exp(s - jnp.max(s, axis=-1, keepdims=True))
        inv = 1.0 / jnp.sum(p, axis=-1, keepdims=True)
        o = jnp.dot(p.astype(mv_ref.dtype), mv_ref[:, cols], preferred_element_type=F32) * inv
        mo.append(o.astype(wm_ref.dtype))
    memo = jnp.concatenate(mo, axis=-1)
    a_out = jnp.dot(attn_ref[...], wa_ref[...], preferred_element_type=F32)
    c_out = jnp.dot(cv, wc_ref[...], preferred_element_type=F32)
    m_out = jnp.dot(memo, wm_ref[...], preferred_element_type=F32)
    merged = (jax.nn.sigmoid(g0_ref[...].astype(F32)) * a_out
              + jax.nn.sigmoid(g1_ref[...].astype(F32)) * c_out
              + jax.nn.sigmoid(g2_ref[...].astype(F32)) * m_out)
    o_ref[...] = merged.astype(o_ref.dtype)


def _mixer(attn, proj, mem_k, mem_v, conv_w, wa, wc, wm, seq_len):
    n = attn.shape[0]
    tt = MIX_T
    assert seq_len % tt == 0
    tiles_per_seq = seq_len // tt
    n8 = n // 8

    def col(c, width):
        return pl.BlockSpec((tt, width), lambda t: (t, c // width))

    def prev8(c):
        return pl.BlockSpec((8, CONV_WIDTH), lambda t: (jnp.maximum(t * (tt // 8) - 1, 0), c // CONV_WIDTH))

    def next8(c):
        return pl.BlockSpec((8, CONV_WIDTH), lambda t: (jnp.minimum((t + 1) * (tt // 8), n8 - 1), c // CONV_WIDTH))

    def mem_spec():
        return pl.BlockSpec((N_MEM, MEM_WIDTH), lambda t: (t // tiles_per_seq, 0))

    return pl.pallas_call(
        functools.partial(_mixer_kernel, tiles_per_seq=tiles_per_seq),
        grid=(n // tt,),
        in_specs=[pl.BlockSpec((tt, ATTN_WIDTH), lambda t: (t, 0)),
                  col(COL_CH, CONV_WIDTH), col(COL_CB, CONV_WIDTH), col(COL_CC, CONV_WIDTH),
                  prev8(COL_CH), prev8(COL_CC), next8(COL_CH), next8(COL_CC),
                  col(COL_GL, D_MODEL), col(COL_GL + D_MODEL, D_MODEL), col(COL_GL + 2 * D_MODEL, D_MODEL),
                  col(COL_MQ, MEM_WIDTH), mem_spec(), mem_spec(),
                  _resident(conv_w.shape), _resident(wa.shape), _resident(wc.shape), _resident(wm.shape)],
        out_specs=pl.BlockSpec((tt, D_MODEL), lambda t: (t, 0)),
        out_shape=jax.ShapeDtypeStruct((n, D_MODEL), attn.dtype),
        compiler_params=_cparams("parallel"),
        name="mixer",
    )(attn, proj, proj, proj, proj, proj, proj, proj, proj, proj, proj, proj, mem_k, mem_v, conv_w, wa, wc, wm)


def _sort_network(n):
    pairs, p = [], 1
    while p < n:
        k = p
        while k >= 1:
            for j in range(k % p, n - k, 2 * k):
                for i in range(min(k, n - j - k)):
                    if (i + j) // (2 * p) == (i + j + k) // (2 * p):
                        pairs.append((i + j, i + j + k))
            k //= 2
        p *= 2
    return pairs


def _compare_exchange(v, i, j):
    v[i], v[j] = jnp.maximum(v[i], v[j]), jnp.minimum(v[i], v[j])


def _top16_rows(v, n_valid):
    assert 2 * n_valid >= PEER_TOPK
    v = list(v)
    for i, j in _sort_network(PEER_TOPK):
        if j < n_valid:
            _compare_exchange(v, i, j)
    for shift in (4, 2, 1):
        other = [pltpu.roll(x, shift, 0) for x in v[:n_valid]]
        merged = []
        for a in range(PEER_TOPK):
            b = PEER_TOPK - 1 - a
            if a < n_valid and b < n_valid:
                merged.append(jnp.maximum(v[a], other[b]))
            else:
                merged.append(v[a] if a < n_valid else other[b])
        v, n_valid = merged, PEER_TOPK
        for d in (8, 4, 2, 1):
            for a in range(PEER_TOPK):
                if a & d == 0:
                    _compare_exchange(v, a, a + d)
    return v


def _rows_to_block(rows, sub):
    out = rows[0]
    for k in range(1, 8):
        out = jnp.where(sub == k, rows[k], out)
    return out


def _all_sublanes(x, op):
    for shift in (4, 2, 1):
        x = op(x, pltpu.roll(x, shift, 0))
    return x


def _route_column(s0, s1):
    n_slab = N_KEYS // 8
    slabs0 = [s0[8 * v:8 * v + 8] for v in range(n_slab)]
    slabs1 = [s1[8 * v:8 * v + 8] for v in range(n_slab)]
    t0 = _top16_rows(slabs0, n_slab)
    t1 = _top16_rows(slabs1, n_slab)
    sub = lax.broadcasted_iota(jnp.int32, slabs0[0].shape, 0)
    v0 = (_rows_to_block(t0[:8], sub), _rows_to_block(t0[8:], sub))
    v1_lo = _rows_to_block(t1[:8], sub)
    cand = [t0[0] + v1_lo, t0[0] + _rows_to_block(t1[8:], sub), t0[1] + v1_lo]
    for a in range(2, 8):
        cand.append(jnp.where(sub < PEER_TOPK // (a + 1), t0[a] + v1_lo, -jnp.inf))
    cand.append(v0[1] + t1[0])
    thr = _top16_rows(cand + [None] * (PEER_TOPK - len(cand)), len(cand))[PEER_TOPK - 1]
    e0 = [jnp.exp(x - t0[0]) for x in v0]
    zacc = [jnp.zeros_like(thr), jnp.zeros_like(thr)]
    lowest = []
    for b in range(PEER_TOPK):
        e1b = jnp.exp(t1[b] - t1[0])
        low = None
        for half in range(2):
            sel = (v0[half] + t1[b]) >= thr
            zacc[half] = zacc[half] + jnp.where(sel, e1b, 0.0)
            cur = jnp.where(sel, v0[half], jnp.inf)
            low = cur if low is None else jnp.minimum(low, cur)
        lowest.append(_all_sublanes(low, jnp.minimum))
    inv_z = 1.0 / _all_sublanes(e0[0] * zacc[0] + e0[1] * zacc[1], jnp.add)
    c0, m0, r1, m1 = [], [], [], []
    for v in range(n_slab):
        c = jnp.zeros_like(slabs0[v])
        for b in range(PEER_TOPK):
            c = jnp.where(slabs0[v] >= lowest[b], float(b + 1), c)
        r = jnp.full_like(slabs1[v], float(PEER_TOPK))
        for a in reversed(range(PEER_TOPK)):
            r = jnp.where(slabs1[v] >= t1[a], float(a), r)
        c0.append(c)
        r1.append(r)
        m0.append(jnp.exp(slabs0[v] - t0[0]) * inv_z)
        m1.append(jnp.exp(slabs1[v] - t1[0]))
    cat = lambda xs: jnp.concatenate(xs, axis=0)
    return cat(c0), cat(m0), cat(r1), cat(m1)


def _ln1_route_kernel(x_ref, mg_ref, wo_ref, g_ref, b_ref, wq_ref, keys_ref,
                      h_ref, ht_ref, c0_ref, m0_ref, r1_ref, m1_ref, qt_scr):
    mix = jnp.dot(mg_ref[...], wo_ref[...], preferred_element_type=F32)
    h = _layer_norm(ALPHA * x_ref[...] + mix, g_ref[...], b_ref[...])
    h_ref[...] = h
    ht = h.T.astype(ht_ref.dtype)
    ht_ref[...] = ht
    qt_scr[...] = jnp.dot(wq_ref[...], ht, preferred_element_type=F32).astype(qt_scr.dtype)

    def route(hd, lanes):
        base = pl.multiple_of(hd * (2 * PEER_HALF), 2 * PEER_HALF)
        s0 = jnp.dot(keys_ref[2 * hd], qt_scr[pl.ds(base, PEER_HALF), lanes], preferred_element_type=F32)
        s1 = jnp.dot(keys_ref[2 * hd + 1], qt_scr[pl.ds(base + PEER_HALF, PEER_HALF), lanes],
                     preferred_element_type=F32)
        c0, m0, r1, m1 = _route_column(s0, s1)
        out_rows = pl.ds(pl.multiple_of(hd * N_KEYS, N_KEYS), N_KEYS)
        c0_ref[out_rows, lanes] = c0
        m0_ref[out_rows, lanes] = m0
        r1_ref[out_rows, lanes] = r1.astype(r1_ref.dtype)
        m1_ref[out_rows, lanes] = m1.astype(m1_ref.dtype)

    def head(hd, carry):
        for col in range(x_ref.shape[0] // ROUTE_LANES):
            route(hd, slice(col * ROUTE_LANES, (col + 1) * ROUTE_LANES))
        return carry

    lax.fori_loop(0, PEER_HEADS, head, 0)


def _ln1_route(x, merged, w_out, ln_g, ln_b, wq_t, keys):
    n = x.shape[0]
    tt = ROUTE_T
    rows = PEER_HEADS * N_KEYS
    tok = lambda t: (t, 0)
    tok_t = lambda t: (0, t)
    return pl.pallas_call(
        _ln1_route_kernel,
        grid=(n // tt,),
        in_specs=[pl.BlockSpec((tt, D_MODEL), tok), pl.BlockSpec((tt, D_MODEL), tok),
                  _resident(w_out.shape), _resident(ln_g.shape), _resident(ln_b.shape),
                  _resident(wq_t.shape), _resident(keys.shape)],
        out_specs=[pl.BlockSpec((tt, D_MODEL), tok), pl.BlockSpec((D_MODEL, tt), tok_t),
                   pl.BlockSpec((rows, tt), tok_t), pl.BlockSpec((rows, tt), tok_t),
                   pl.BlockSpec((rows, tt), tok_t), pl.BlockSpec((rows, tt), tok_t)],
        out_shape=[jax.ShapeDtypeStruct((n, D_MODEL), F32),
                   jax.ShapeDtypeStruct((D_MODEL, n), MXU_DT),
                   jax.ShapeDtypeStruct((rows, n), F32), jax.ShapeDtypeStruct((rows, n), F32),
                   jax.ShapeDtypeStruct((rows, n), GATE_DT), jax.ShapeDtypeStruct((rows, n), GATE_DT)],
        scratch_shapes=[pltpu.VMEM((PEER_HEADS * 2 * PEER_HALF, tt), MXU_DT)],
        compiler_params=_cparams("parallel"),
        name="ln1_route",
    )(x, merged, w_out, ln_g, ln_b, wq_t, keys)


def _peer_gate_block(block, pre, c0_ref, m0_ref, r1_ref, m1_ref):
    act = (0.5 * pre * (1.0 + lax.erf(pre * math.sqrt(0.5)))).astype(GATE_DT)
    per_block = pre.shape[0] // N_KEYS
    out = []
    for ii in range(per_block):
        i = block * per_block + ii
        gate = None
        for hd in range(PEER_HEADS):
            c0 = c0_ref[pl.ds(hd * N_KEYS + i, 1), :].astype(GATE_DT)
            m0 = m0_ref[pl.ds(hd * N_KEYS + i, 1), :].astype(GATE_DT)
            r1 = r1_ref[hd * N_KEYS:(hd + 1) * N_KEYS, :]
            m1 = m1_ref[hd * N_KEYS:(hd + 1) * N_KEYS, :]
            term = jnp.where(r1 < c0, m1, jnp.zeros_like(m1)) * m0
            gate = term if gate is None else gate + term
        out.append((gate * act[ii * N_KEYS:(ii + 1) * N_KEYS, :]).astype(MXU_DT))
    return jnp.concatenate(out, axis=0)


def _peer_kernel(ht_ref, u_ref, vt_ref, c0_ref, m0_ref, r1_ref, m1_ref, o_ref):
    g = pl.program_id(1)
    eb = PEER_EB
    n_sub = u_ref.shape[0] // eb

    @pl.when(g == 0)
    def _():
        o_ref[...] = jnp.zeros_like(o_ref)

    routing = (c0_ref, m0_ref, r1_ref, m1_ref)
    pre = [jnp.dot(u_ref[k * eb:(k + 1) * eb, :], ht_ref[...], preferred_element_type=F32)
           for k in range(n_sub)]
    for k in range(n_sub):
        w = _peer_gate_block(g * n_sub + k, pre[k], *routing)
        o_ref[...] += jnp.dot(vt_ref[:, k * eb:(k + 1) * eb], w, preferred_element_type=F32)


def _peer_dense(ht, u, vt, c0, m0, r1, m1):
    n = ht.shape[1]
    c, step = PEER_C, PEER_EB * PEER_SUB
    rows = PEER_HEADS * N_KEYS
    tok_t = lambda t, g: (0, t)
    tile = lambda r: pl.BlockSpec((r, c), tok_t, pipeline_mode=pl.Buffered(1))
    return pl.pallas_call(
        _peer_kernel,
        grid=(n // c, N_EXPERTS // step),
        in_specs=[tile(D_MODEL),
                  pl.BlockSpec((step, D_MODEL), lambda t, g: (g, 0)),
                  pl.BlockSpec((D_MODEL, step), lambda t, g: (0, g)),
                  tile(rows), tile(rows), tile(rows), tile(rows)],
        out_specs=tile(D_MODEL),
        out_shape=jax.ShapeDtypeStruct((D_MODEL, n), F32),
        compiler_params=_cparams("parallel", "arbitrary"),
        name="peer_dense",
    )(ht, u, vt, c0, m0, r1, m1)


def _ln2_kernel(h_ref, pt_ref, g_ref, b_ref, o_ref):
    o_ref[...] = _layer_norm(ALPHA * h_ref[...] + pt_ref[...].T, g_ref[...], b_ref[...])


def _ln2(h, peer_t, ln_g, ln_b):
    n = h.shape[0]
    tt = LN2_T
    return pl.pallas_call(
        _ln2_kernel,
        grid=(n // tt,),
        in_specs=[pl.BlockSpec((tt, D_MODEL), lambda t: (t, 0)),
                  pl.BlockSpec((D_MODEL, tt), lambda t: (0, t)),
                  _resident(ln_g.shape), _resident(ln_b.shape)],
        out_specs=pl.BlockSpec((tt, D_MODEL), lambda t: (t, 0)),
        out_shape=jax.ShapeDtypeStruct((n, D_MODEL), F32),
        compiler_params=_cparams("parallel"),
        name="ln2",
    )(h, peer_t, ln_g, ln_b)


def _prepare(w_in, sink, conv_w, w_mem_k, w_mem_v, w_attn_o, w_conv_out, w_mem_o, w_out,
             ln1_g, ln1_b, w_peer_q, peer_keys, peer_u, peer_v, ln2_g, ln2_b):
    cuts, off = {}, 0
    for name, width in (("q", ATTN_WIDTH), ("k", KV_WIDTH), ("v", KV_WIDTH), ("ch", CONV_WIDTH),
                        ("cb", CONV_WIDTH), ("cc", CONV_WIDTH), ("mq", MEM_WIDTH), ("gl", GATE_WIDTH)):
        cuts[name] = w_in[:, off:off + width]
        off += width
    w_in_perm = jnp.concatenate([cuts[k] for k in ("q", "ch", "cb", "cc", "gl", "mq", "k", "v")], axis=1)
    row = lambda v: v.reshape(1, -1).astype(F32)
    return dict(
        w_in=w_in_perm.astype(MXU_DT), sink=sink.astype(F32), conv_w=conv_w.astype(F32),
        w_mem_k=w_mem_k.astype(MXU_DT), w_mem_v=w_mem_v.astype(MXU_DT),
        wa=w_attn_o.astype(MXU_DT), wc=w_conv_out.astype(MXU_DT), wm=w_mem_o.astype(MXU_DT),
        w_out=w_out.astype(MXU_DT), ln1_g=row(ln1_g), ln1_b=row(ln1_b),
        wq_t=w_peer_q.T.astype(MXU_DT),
        keys=peer_keys.reshape(PEER_HEADS * 2, N_KEYS, PEER_HALF).astype(MXU_DT),
        u=peer_u.astype(MXU_DT), vt=peer_v.T.astype(MXU_DT), ln2_g=row(ln2_g), ln2_b=row(ln2_b))


def _encode_group(x, mem, p):
    b, s, d = x.shape
    x2 = x.reshape(b * s, d)
    mem2 = mem.reshape(b * N_MEM, d)
    proj = _matmul(x2, p["w_in"], MXU_DT, PROJ_TM, PROJ_TN)
    mem_k = _matmul(mem2, p["w_mem_k"], MXU_DT, 512, 512)
    mem_v = _matmul(mem2, p["w_mem_v"], MXU_DT, 512, 512)
    attn = _attention(proj, p["sink"], s)
    merged = _mixer(attn, proj, mem_k, mem_v, p["conv_w"], p["wa"], p["wc"], p["wm"], s)
    h, ht, c0, m0, r1, m1 = _ln1_route(x2, merged, p["w_out"], p["ln1_g"], p["ln1_b"], p["wq_t"], p["keys"])
    peer_t = _peer_dense(ht, p["u"], p["vt"], c0, m0, r1, m1)
    return _ln2(h, peer_t, p["ln2_g"], p["ln2_b"]).reshape(b, s, d)


def kernel(x_prompt, x_sample, mem_prompt, mem_sample, w_in, sink, conv_w, w_mem_k, w_mem_v, w_attn_o,
           w_conv_out, w_mem_o, w_out, ln1_g, ln1_b, w_peer_q, peer_keys, peer_u, peer_v, ln2_g, ln2_b):
    assert w_in.shape[0] == DEPTH == 1
    p = _prepare(w_in[0], sink[0], conv_w[0], w_mem_k[0], w_mem_v[0], w_attn_o[0], w_conv_out[0],
                 w_mem_o[0], w_out[0], ln1_g[0], ln1_b[0], w_peer_q[0], peer_keys[0], peer_u[0],
                 peer_v[0], ln2_g[0], ln2_b[0])
    return (_encode_group(x_prompt, mem_prompt, p), _encode_group(x_sample, mem_sample, p))
```

```python
import functools
import math

import jax
import jax.numpy as jnp
from jax import lax
from jax.experimental import pallas as pl
from jax.experimental.pallas import tpu as pltpu

D_MODEL = 2048
N_Q_HEADS = 8
N_KV_HEADS = 2
GQA_GROUP = N_Q_HEADS // N_KV_HEADS
HEAD_DIM = 128
WINDOW = 128
ATTN_WIDTH = N_Q_HEADS * HEAD_DIM
KV_WIDTH = N_KV_HEADS * HEAD_DIM
CONV_WIDTH = 1024
N_MEM = 256
MEM_HEADS = 4
MEM_HEAD_DIM = 256
MEM_WIDTH = MEM_HEADS * MEM_HEAD_DIM
PEER_HEADS = 8
N_KEYS = 128
N_EXPERTS = N_KEYS * N_KEYS
PEER_HALF = 128
PEER_TOPK = 16
LN_EPS = 1e-5
DEPTH = 1
ALPHA = (2.0 * DEPTH) ** 0.25
GATE_WIDTH = 3 * D_MODEL
IN_WIDTH = ATTN_WIDTH + 2 * KV_WIDTH + 3 * CONV_WIDTH + MEM_WIDTH + GATE_WIDTH

COL_Q = 0
COL_CH = COL_Q + ATTN_WIDTH
COL_CB = COL_CH + CONV_WIDTH
COL_CC = COL_CB + CONV_WIDTH
COL_GL = COL_CC + CONV_WIDTH
COL_MQ = COL_GL + GATE_WIDTH
COL_K = COL_MQ + MEM_WIDTH
COL_V = COL_K + KV_WIDTH

MXU_DT = jnp.bfloat16
GATE_DT = jnp.bfloat16
VMEM_LIMIT = 56 * 1024 * 1024

PROJ_TM = 1024
PROJ_TN = 512
ATTN_BLOCKS = 4
MIX_T = 256
ROUTE_T = 256
ROUTE_LANES = 128
PEER_C = 512
PEER_EB = 256
PEER_SUB = 8
LN2_T = 256

F32 = jnp.float32


def _cparams(*sem):
    return pltpu.CompilerParams(dimension_semantics=sem, vmem_limit_bytes=VMEM_LIMIT)


def _resident(shape):
    nd = len(shape)
    return pl.BlockSpec(shape, lambda *_: (0,) * nd, pipeline_mode=pl.Buffered(1))


def _layer_norm(x, g, b):
    mu = jnp.mean(x, axis=-1, keepdims=True)
    xc = x - mu
    var = jnp.mean(xc * xc, axis=-1, keepdims=True)
    return xc * lax.rsqrt(var + LN_EPS) * g + b


def _mm_kernel(x_ref, w_ref, o_ref, xb_ref):
    @pl.when(pl.program_id(1) == 0)
    def _():
        xb_ref[...] = x_ref[...].astype(xb_ref.dtype)

    o_ref[...] = jnp.dot(xb_ref[...], w_ref[...], preferred_element_type=F32).astype(o_ref.dtype)


def _matmul(x, w, out_dtype, tm, tn):
    m, k = x.shape
    n = w.shape[1]
    tm, tn = min(tm, m), min(tn, n)
    assert m % tm == 0 and n % tn == 0
    return pl.pallas_call(
        _mm_kernel,
        grid=(m // tm, n // tn),
        in_specs=[pl.BlockSpec((tm, k), lambda i, j: (i, 0)),
                  pl.BlockSpec((k, tn), lambda i, j: (0, j))],
        out_specs=pl.BlockSpec((tm, tn), lambda i, j: (i, j)),
        out_shape=jax.ShapeDtypeStruct((m, n), out_dtype),
        scratch_shapes=[pltpu.VMEM((tm, k), w.dtype)],
        compiler_params=_cparams("parallel", "arbitrary"),
        name="matmul",
    )(x, w)


def _attn_kernel(sink_ref, q_ref, kp_ref, km_ref, kn_ref, vp_ref, vm_ref, vn_ref, o_ref, *, tiles_per_seq):
    t = pl.program_id(0)
    pos = t % tiles_per_seq
    lo = jnp.where(pos == 0, WINDOW, 0)
    hi = jnp.where(pos == tiles_per_seq - 1, 2 * WINDOW, 3 * WINDOW)
    kall = jnp.concatenate([kp_ref[...], km_ref[...], kn_ref[...]], axis=0)
    vall = jnp.concatenate([vp_ref[...], vm_ref[...], vn_ref[...]], axis=0)
    qi = lax.broadcasted_iota(jnp.int32, (WINDOW, 3 * WINDOW), 0)
    si = lax.broadcasted_iota(jnp.int32, (WINDOW, 3 * WINDOW), 1)
    dist = jnp.abs(qi + WINDOW - si)
    band = dist <= WINDOW
    distf = dist.astype(F32)
    scale = 1.0 / math.sqrt(HEAD_DIM)
    n_blocks = q_ref.shape[0] // WINDOW
    for r in range(n_blocks):
        valid = band
        if r == 0:
            valid = valid & (si >= lo)
        if r == n_blocks - 1:
            valid = valid & (si < hi)
        rows = slice(r * WINDOW, (r + 1) * WINDOW)
        for kv in range(N_KV_HEADS):
            cols = slice(kv * HEAD_DIM, (kv + 1) * HEAD_DIM)
            kc = kall[r * WINDOW:(r + 3) * WINDOW, cols]
            vc = vall[r * WINDOW:(r + 3) * WINDOW, cols]
            heads = [kv * GQA_GROUP + g for g in range(GQA_GROUP)]
            q4 = jnp.concatenate([q_ref[rows, h * HEAD_DIM:(h + 1) * HEAD_DIM] for h in heads], axis=0)
            s4 = lax.dot_general(q4, kc, (((1,), (1,)), ((), ())), preferred_element_type=F32) * scale
            ps, inv = [], []
            for g, h in enumerate(heads):
                slope = 2.0 ** (-8.0 * (h + 1) / N_Q_HEADS)
                s = jnp.where(valid, s4[g * WINDOW:(g + 1) * WINDOW] - slope * distf, -jnp.inf)
                sk = sink_ref[h]
                m = jnp.maximum(jnp.max(s, axis=-1, keepdims=True), sk)
                p = jnp.exp(s - m)
                denom = jnp.sum(p, axis=-1, keepdims=True) + jnp.exp(sk - m)
                ps.append(p.astype(vc.dtype))
                inv.append(1.0 / denom)
            o4 = jnp.dot(jnp.concatenate(ps, axis=0), vc, preferred_element_type=F32)
            for g, h in enumerate(heads):
                o_ref[rows, h * HEAD_DIM:(h + 1) * HEAD_DIM] = (
                    o4[g * WINDOW:(g + 1) * WINDOW] * inv[g]).astype(o_ref.dtype)


def _attention(proj, sink, seq_len):
    n = proj.shape[0]
    ta = ATTN_BLOCKS * WINDOW
    assert seq_len % ta == 0 and n % seq_len == 0
    nblk = n // WINDOW
    kcol, vcol = COL_K // KV_WIDTH, COL_V // KV_WIDTH

    def halo(col, side):
        if side < 0:
            return pl.BlockSpec((WINDOW, KV_WIDTH), lambda t: (jnp.maximum(t * ATTN_BLOCKS - 1, 0), col))
        return pl.BlockSpec((WINDOW, KV_WIDTH), lambda t: (jnp.minimum((t + 1) * ATTN_BLOCKS, nblk - 1), col))

    def main(col):
        return pl.BlockSpec((ta, KV_WIDTH), lambda t: (t, col))

    return pl.pallas_call(
        functools.partial(_attn_kernel, tiles_per_seq=seq_len // ta),
        grid=(n // ta,),
        in_specs=[pl.BlockSpec(memory_space=pltpu.SMEM),
                  pl.BlockSpec((ta, ATTN_WIDTH), lambda t: (t, COL_Q // ATTN_WIDTH)),
                  halo(kcol, -1), main(kcol), halo(kcol, 1),
                  halo(vcol, -1), main(vcol), halo(vcol, 1)],
        out_specs=pl.BlockSpec((ta, ATTN_WIDTH), lambda t: (t, 0)),
        out_shape=jax.ShapeDtypeStruct((n, ATTN_WIDTH), proj.dtype),
        compiler_params=_cparams("parallel"),
        name="attention",
    )(sink, proj, proj, proj, proj, proj, proj, proj)


def _mixer_kernel(attn_ref, ch_ref, cb_ref, cc_ref, chp_ref, ccp_ref, chn_ref, ccn_ref,
                  g0_ref, g1_ref, g2_ref, mq_ref, mk_ref, mv_ref, convw_ref,
                  wa_ref, wc_ref, wm_ref, o_ref, *, tiles_per_seq):
    t = pl.program_id(0)
    pos = t % tiles_per_seq
    tt = ch_ref.shape[0]
    z = cc_ref[...].astype(F32) * ch_ref[...].astype(F32)
    zp = ccp_ref[7:8, :].astype(F32) * chp_ref[7:8, :].astype(F32)
    zn = ccn_ref[0:1, :].astype(F32) * chn_ref[0:1, :].astype(F32)
    zp = jnp.where(pos == 0, 0.0, zp)
    zn = jnp.where(pos == tiles_per_seq - 1, 0.0, zn)
    row = lax.broadcasted_iota(jnp.int32, z.shape, 0)
    z_up = jnp.where(row == 0, zp, pltpu.roll(z, 1, 0))
    z_dn = jnp.where(row == tt - 1, zn, pltpu.roll(z, tt - 1, 0))
    conv = z_up * convw_ref[0:1, :] + z * convw_ref[1:2, :] + z_dn * convw_ref[2:3, :]
    cv = (cb_ref[...].astype(F32) * conv).astype(wc_ref.dtype)
    mscale = 1.0 / math.sqrt(MEM_HEAD_DIM)
    mo = []
    for hm in range(MEM_HEADS):
        cols = slice(hm * MEM_HEAD_DIM, (hm + 1) * MEM_HEAD_DIM)
        s = lax.dot_general(mq_ref[:, cols], mk_ref[:, cols], (((1,), (1,)), ((), ())),
                            preferred_element_type=F32) * mscale
        p = jnp.exp(s - jnp.max(s, axis=-1, keepdims=True))
        inv = 1.0 / jnp.sum(p, axis=-1, keepdims=True)
        o = jnp.dot(p.astype(mv_ref.dtype), mv_ref[:, cols], preferred_element_type=F32) * inv
        mo.append(o.astype(wm_ref.dtype))
    memo = jnp.concatenate(mo, axis=-1)
    a_out = jnp.dot(attn_ref[...], wa_ref[...], preferred_element_type=F32)
    c_out = jnp.dot(cv, wc_ref[...], preferred_element_type=F32)
    m_out = jnp.dot(memo, wm_ref[...], preferred_element_type=F32)
    merged = (jax.nn.sigmoid(g0_ref[...].astype(F32)) * a_out
              + jax.nn.sigmoid(g1_ref[...].astype(F32)) * c_out
              + jax.nn.sigmoid(g2_ref[...].astype(F32)) * m_out)
    o_ref[...] = merged.astype(o_ref.dtype)


def _mixer(attn, proj, mem_k, mem_v, conv_w, wa, wc, wm, seq_len):
    n = attn.shape[0]
    tt = MIX_T
    assert seq_len % tt == 0
    tiles_per_seq = seq_len // tt
    n8 = n // 8

    def col(c, width):
        return pl.BlockSpec((tt, width), lambda t: (t, c // width))

    def prev8(c):
        return pl.BlockSpec((8, CONV_WIDTH), lambda t: (jnp.maximum(t * (tt // 8) - 1, 0), c // CONV_WIDTH))

    def next8(c):
        return pl.BlockSpec((8, CONV_WIDTH), lambda t: (jnp.minimum((t + 1) * (tt // 8), n8 - 1), c // CONV_WIDTH))

    def mem_spec():
        return pl.BlockSpec((N_MEM, MEM_WIDTH), lambda t: (t // tiles_per_seq, 0))

    return pl.pallas_call(
        functools.partial(_mixer_kernel, tiles_per_seq=tiles_per_seq),
        grid=(n // tt,),
        in_specs=[pl.BlockSpec((tt, ATTN_WIDTH), lambda t: (t, 0)),
                  col(COL_CH, CONV_WIDTH), col(COL_CB, CONV_WIDTH), col(COL_CC, CONV_WIDTH),
                  prev8(COL_CH), prev8(COL_CC), next8(COL_CH), next8(COL_CC),
                  col(COL_GL, D_MODEL), col(COL_GL + D_MODEL, D_MODEL), col(COL_GL + 2 * D_MODEL, D_MODEL),
                  col(COL_MQ, MEM_WIDTH), mem_spec(), mem_spec(),
                  _resident(conv_w.shape), _resident(wa.shape), _resident(wc.shape), _resident(wm.shape)],
        out_specs=pl.BlockSpec((tt, D_MODEL), lambda t: (t, 0)),
        out_shape=jax.ShapeDtypeStruct((n, D_MODEL), attn.dtype),
        compiler_params=_cparams("parallel"),
        name="mixer",
    )(attn, proj, proj, proj, proj, proj, proj, proj, proj, proj, proj, proj, mem_k, mem_v, conv_w, wa, wc, wm)


def _sort_network(n):
    pairs, p = [], 1
    while p < n:
        k = p
        while k >= 1:
            for j in range(k % p, n - k, 2 * k):
                for i in range(min(k, n - j - k)):
                    if (i + j) // (2 * p) == (i + j + k) // (2 * p):
                        pairs.append((i + j, i + j + k))
            k //= 2
        p *= 2
    return pairs


def _compare_exchange(v, i, j):
    v[i], v[j] = jnp.maximum(v[i], v[j]), jnp.minimum(v[i], v[j])


def _top16_rows(v, n_valid):
    assert 2 * n_valid >= PEER_TOPK
    v = list(v)
    for i, j in _sort_network(PEER_TOPK):
        if j < n_valid:
            _compare_exchange(v, i, j)
    for shift in (4, 2, 1):
        other = [pltpu.roll(x, shift, 0) for x in v[:n_valid]]
        merged = []
        for a in range(PEER_TOPK):
            b = PEER_TOPK - 1 - a
            if a < n_valid and b < n_valid:
                merged.append(jnp.maximum(v[a], other[b]))
            else:
                merged.append(v[a] if a < n_valid else other[b])
        v, n_valid = merged, PEER_TOPK
        for d in (8, 4, 2, 1):
            for a in range(PEER_TOPK):
                if a & d == 0:
                    _compare_exchange(v, a, a + d)
    return v


def _rows_to_block(rows, sub):
    out = rows[0]
    for k in range(1, 8):
        out = jnp.where(sub == k, rows[k], out)
    return out


def _all_sublanes(x, op):
    for shift in (4, 2, 1):
        x = op(x, pltpu.roll(x, shift, 0))
    return x


def _route_column(s0, s1):
    n_slab = N_KEYS // 8
    slabs0 = [s0[8 * v:8 * v + 8] for v in range(n_slab)]
    slabs1 = [s1[8 * v:8 * v + 8] for v in range(n_slab)]
    t0 = _top16_rows(slabs0, n_slab)
    t1 = _top16_rows(slabs1, n_slab)
    sub = lax.broadcasted_iota(jnp.int32, slabs0[0].shape, 0)
    v0 = (_rows_to_block(t0[:8], sub), _rows_to_block(t0[8:], sub))
    v1_lo = _rows_to_block(t1[:8], sub)
    cand = [t0[0] + v1_lo, t0[0] + _rows_to_block(t1[8:], sub), t0[1] + v1_lo]
    for a in range(2, 8):
        cand.append(jnp.where(sub < PEER_TOPK // (a + 1), t0[a] + v1_lo, -jnp.inf))
    cand.append(v0[1] + t1[0])
    thr = _top16_rows(cand + [None] * (PEER_TOPK - len(cand)), len(cand))[PEER_TOPK - 1]
    e0 = [jnp.exp(x - t0[0]) for x in v0]
    zacc = [jnp.zeros_like(thr), jnp.zeros_like(thr)]
    lowest = []
    for b in range(PEER_TOPK):
        e1b = jnp.exp(t1[b] - t1[0])
        low = None
        for half in range(2):
            sel = (v0[half] + t1[b]) >= thr
            zacc[half] = zacc[half] + jnp.where(sel, e1b, 0.0)
            cur = jnp.where(sel, v0[half], jnp.inf)
            low = cur if low is None else jnp.minimum(low, cur)
        lowest.append(_all_sublanes(low, jnp.minimum))
    inv_z = 1.0 / _all_sublanes(e0[0] * zacc[0] + e0[1] * zacc[1], jnp.add)
    c0, m0, r1, m1 = [], [], [], []
    for v in range(n_slab):
        c = jnp.zeros_like(slabs0[v])
        for b in range(PEER_TOPK):
            c = jnp.where(slabs0[v] >= lowest[b], float(b + 1), c)
        r = jnp.full_like(slabs1[v], float(PEER_TOPK))
        for a in reversed(range(PEER_TOPK)):
            r = jnp.where(slabs1[v] >= t1[a], float(a), r)
        c0.append(c)
        r1.append(r)
        m0.append(jnp.exp(slabs0[v] - t0[0]) * inv_z)
        m1.append(jnp.exp(slabs1[v] - t1[0]))
    cat = lambda xs: jnp.concatenate(xs, axis=0)
    return cat(c0), cat(m0), cat(r1), cat(m1)


COUNT_MASK = 0xFF


def _pack_count_factor(count, factor):
    top = lax.bitcast_convert_type(factor.astype(GATE_DT).astype(F32), jnp.uint32)
    return top | count.astype(jnp.int32).astype(jnp.uint32)


def _ln1_route_kernel(x_ref, mg_ref, wo_ref, g_ref, b_ref, wq_ref, keys_ref,
                      h_ref, ht_ref, cm_ref, r1_ref, m1_ref, qt_scr):
    mix = jnp.dot(mg_ref[...], wo_ref[...], preferred_element_type=F32)
    h = _layer_norm(ALPHA * x_ref[...] + mix, g_ref[...], b_ref[...])
    h_ref[...] = h
    ht = h.T.astype(ht_ref.dtype)
    ht_ref[...] = ht
    qt_scr[...] = jnp.dot(wq_ref[...], ht, preferred_element_type=F32).astype(qt_scr.dtype)

    def route(hd, lanes):
        base = pl.multiple_of(hd * (2 * PEER_HALF), 2 * PEER_HALF)
        s0 = jnp.dot(keys_ref[2 * hd], qt_scr[pl.ds(base, PEER_HALF), lanes], preferred_element_type=F32)
        s1 = jnp.dot(keys_ref[2 * hd + 1], qt_scr[pl.ds(base + PEER_HALF, PEER_HALF), lanes],
                     preferred_element_type=F32)
        c0, m0, r1, m1 = _route_column(s0, s1)
        out_rows = pl.ds(pl.multiple_of(hd * N_KEYS, N_KEYS), N_KEYS)
        cm_ref[out_rows, lanes] = _pack_count_factor(c0, m0)
        r1_ref[out_rows, lanes] = r1.astype(r1_ref.dtype)
        m1_ref[out_rows, lanes] = m1.astype(m1_ref.dtype)

    def head(hd, carry):
        for col in range(x_ref.shape[0] // ROUTE_LANES):
            route(hd, slice(col * ROUTE_LANES, (col + 1) * ROUTE_LANES))
        return carry

    lax.fori_loop(0, PEER_HEADS, head, 0)


def _ln1_route(x, merged, w_out, ln_g, ln_b, wq_t, keys):
    n = x.shape[0]
    tt = ROUTE_T
    rows = PEER_HEADS * N_KEYS
    tok = lambda t: (t, 0)
    tok_t = lambda t: (0, t)
    return pl.pallas_call(
        _ln1_route_kernel,
        grid=(n // tt,),
        in_specs=[pl.BlockSpec((tt, D_MODEL), tok), pl.BlockSpec((tt, D_MODEL), tok),
                  _resident(w_out.shape), _resident(ln_g.shape), _resident(ln_b.shape),
                  _resident(wq_t.shape), _resident(keys.shape)],
        out_specs=[pl.BlockSpec((tt, D_MODEL), tok), pl.BlockSpec((D_MODEL, tt), tok_t),
                   pl.BlockSpec((rows, tt), tok_t), pl.BlockSpec((rows, tt), tok_t),
                   pl.BlockSpec((rows, tt), tok_t)],
        out_shape=[jax.ShapeDtypeStruct((n, D_MODEL), F32),
                   jax.ShapeDtypeStruct((D_MODEL, n), MXU_DT),
                   jax.ShapeDtypeStruct((rows, n), jnp.uint32),
                   jax.ShapeDtypeStruct((rows, n), GATE_DT), jax.ShapeDtypeStruct((rows, n), GATE_DT)],
        scratch_shapes=[pltpu.VMEM((PEER_HEADS * 2 * PEER_HALF, tt), MXU_DT)],
        compiler_params=_cparams("parallel"),
        name="ln1_route",
    )(x, merged, w_out, ln_g, ln_b, wq_t, keys)


def _peer_gate_block(block, pre, cm_ref, r1_ref, m1_ref):
    act = (0.5 * pre * (1.0 + lax.erf(pre * math.sqrt(0.5)))).astype(GATE_DT)
    per_block = pre.shape[0] // N_KEYS
    out = []
    for ii in range(per_block):
        i = block * per_block + ii
        gate = None
        for hd in range(PEER_HEADS):
            word = cm_ref[pl.ds(hd * N_KEYS + i, 1), :]
            m0 = lax.bitcast_convert_type(word, F32).astype(GATE_DT)
            c0 = (word & COUNT_MASK).astype(jnp.int32).astype(F32).astype(GATE_DT)
            r1 = r1_ref[hd * N_KEYS:(hd + 1) * N_KEYS, :]
            m1 = m1_ref[hd * N_KEYS:(hd + 1) * N_KEYS, :]
            term = jnp.where(r1 < c0, m1, jnp.zeros_like(m1)) * m0
            gate = term if gate is None else gate + term
        out.append((gate * act[ii * N_KEYS:(ii + 1) * N_KEYS, :]).astype(MXU_DT))
    return jnp.concatenate(out, axis=0)


def _peer_kernel(ht_ref, u_ref, vt_ref, cm_ref, r1_ref, m1_ref, o_ref):
    g = pl.program_id(1)
    eb = PEER_EB
    n_sub = u_ref.shape[0] // eb

    @pl.when(g == 0)
    def _():
        o_ref[...] = jnp.zeros_like(o_ref)

    routing = (cm_ref, r1_ref, m1_ref)
    pre = [jnp.dot(u_ref[k * eb:(k + 1) * eb, :], ht_ref[...], preferred_element_type=F32)
           for k in range(n_sub)]
    for k in range(n_sub):
        w = _peer_gate_block(g * n_sub + k, pre[k], *routing)
        o_ref[...] += jnp.dot(vt_ref[:, k * eb:(k + 1) * eb], w, preferred_element_type=F32)


def _peer_dense(ht, u, vt, cm, r1, m1):
    n = ht.shape[1]
    c, step = PEER_C, PEER_EB * PEER_SUB
    rows = PEER_HEADS * N_KEYS
    tok_t = lambda t, g: (0, t)
    tile = lambda r: pl.BlockSpec((r, c), tok_t)
    return pl.pallas_call(
        _peer_kernel,
        grid=(n // c, N_EXPERTS // step),
        in_specs=[tile(D_MODEL),
                  pl.BlockSpec((step, D_MODEL), lambda t, g: (g, 0)),
                  pl.BlockSpec((D_MODEL, step), lambda t, g: (0, g)),
                  tile(rows), tile(rows), tile(rows)],
        out_specs=pl.BlockSpec((D_MODEL, c), tok_t, pipeline_mode=pl.Buffered(1)),
        out_shape=jax.ShapeDtypeStruct((D_MODEL, n), F32),
        compiler_params=_cparams("parallel", "arbitrary"),
        name="peer_dense",
    )(ht, u, vt, cm, r1, m1)


def _ln2_kernel(h_ref, pt_ref, g_ref, b_ref, o_ref):
    o_ref[...] = _layer_norm(ALPHA * h_ref[...] + pt_ref[...].T, g_ref[...], b_ref[...])


def _ln2(h, peer_t, ln_g, ln_b):
    n = h.shape[0]
    tt = LN2_T
    return pl.pallas_call(
        _ln2_kernel,
        grid=(n // tt,),
        in_specs=[pl.BlockSpec((tt, D_MODEL), lambda t: (t, 0)),
                  pl.BlockSpec((D_MODEL, tt), lambda t: (0, t)),
                  _resident(ln_g.shape), _resident(ln_b.shape)],
        out_specs=pl.BlockSpec((tt, D_MODEL), lambda t: (t, 0)),
        out_shape=jax.ShapeDtypeStruct((n, D_MODEL), F32),
        compiler_params=_cparams("parallel"),
        name="ln2",
    )(h, peer_t, ln_g, ln_b)


def _prepare(w_in, sink, conv_w, w_mem_k, w_mem_v, w_attn_o, w_conv_out, w_mem_o, w_out,
             ln1_g, ln1_b, w_peer_q, peer_keys, peer_u, peer_v, ln2_g, ln2_b):
    cuts, off = {}, 0
    for name, width in (("q", ATTN_WIDTH), ("k", KV_WIDTH), ("v", KV_WIDTH), ("ch", CONV_WIDTH),
                        ("cb", CONV_WIDTH), ("cc", CONV_WIDTH), ("mq", MEM_WIDTH), ("gl", GATE_WIDTH)):
        cuts[name] = w_in[:, off:off + width]
        off += width
    w_in_perm = jnp.concatenate([cuts[k] for k in ("q", "ch", "cb", "cc", "gl", "mq", "k", "v")], axis=1)
    row = lambda v: v.reshape(1, -1).astype(F32)
    return dict(
        w_in=w_in_perm.astype(MXU_DT), sink=sink.astype(F32), conv_w=conv_w.astype(F32),
        w_mem_k=w_mem_k.astype(MXU_DT), w_mem_v=w_mem_v.astype(MXU_DT),
        wa=w_attn_o.astype(MXU_DT), wc=w_conv_out.astype(MXU_DT), wm=w_mem_o.astype(MXU_DT),
        w_out=w_out.astype(MXU_DT), ln1_g=row(ln1_g), ln1_b=row(ln1_b),
        wq_t=w_peer_q.T.astype(MXU_DT),
        keys=peer_keys.reshape(PEER_HEADS * 2, N_KEYS, PEER_HALF).astype(MXU_DT),
        u=peer_u.astype(MXU_DT), vt=peer_v.T.astype(MXU_DT), ln2_g=row(ln2_g), ln2_b=row(ln2_b))


def _encode_group(x, mem, p):
    b, s, d = x.shape
    x2 = x.reshape(b * s, d)
    mem2 = mem.reshape(b * N_MEM, d)
    proj = _matmul(x2, p["w_in"], MXU_DT, PROJ_TM, PROJ_TN)
    mem_k = _matmul(mem2, p["w_mem_k"], MXU_DT, 512, 512)
    mem_v = _matmul(mem2, p["w_mem_v"], MXU_DT, 512, 512)
    attn = _attention(proj, p["sink"], s)
    merged = _mixer(attn, proj, mem_k, mem_v, p["conv_w"], p["wa"], p["wc"], p["wm"], s)
    h, ht, cm, r1, m1 = _ln1_route(x2, merged, p["w_out"], p["ln1_g"], p["ln1_b"], p["wq_t"], p["keys"])
    peer_t = _peer_dense(ht, p["u"], p["vt"], cm, r1, m1)
    return _ln2(h, peer_t, p["ln2_g"], p["ln2_b"]).reshape(b, s, d)


def kernel(x_prompt, x_sample, mem_prompt, mem_sample, w_in, sink, conv_w, w_mem_k, w_mem_v, w_attn_o,
           w_conv_out, w_mem_o, w_out, ln1_g, ln1_b, w_peer_q, peer_keys, peer_u, peer_v, ln2_g, ln2_b):
    assert w_in.shape[0] == DEPTH == 1
    p = _prepare(w_in[0], sink[0], conv_w[0], w_mem_k[0], w_mem_v[0], w_attn_o[0], w_conv_out[0],
                 w_mem_o[0], w_out[0], ln1_g[0], ln1_b[0], w_peer_q[0], peer_keys[0], peer_u[0],
                 peer_v[0], ln2_g[0], ln2_b[0])
    return (_encode_group(x_prompt, mem_prompt, p), _encode_group(x_sample, mem_sample, p))
```

```python
import functools
import math

import jax
import jax.numpy as jnp
from jax import lax
from jax.experimental import pallas as pl
from jax.experimental.pallas import tpu as pltpu

D_MODEL = 2048
N_Q_HEADS = 8
N_KV_HEADS = 2
GQA_GROUP = N_Q_HEADS // N_KV_HEADS
HEAD_DIM = 128
WINDOW = 128
ATTN_WIDTH = N_Q_HEADS * HEAD_DIM
KV_WIDTH = N_KV_HEADS * HEAD_DIM
CONV_WIDTH = 1024
N_MEM = 256
MEM_HEADS = 4
MEM_HEAD_DIM = 256
MEM_WIDTH = MEM_HEADS * MEM_HEAD_DIM
PEER_HEADS = 8
N_KEYS = 128
N_EXPERTS = N_KEYS * N_KEYS
PEER_HALF = 128
PEER_TOPK = 16
LN_EPS = 1e-5
DEPTH = 1
ALPHA = (2.0 * DEPTH) ** 0.25
GATE_WIDTH = 3 * D_MODEL
IN_WIDTH = ATTN_WIDTH + 2 * KV_WIDTH + 3 * CONV_WIDTH + MEM_WIDTH + GATE_WIDTH

COL_Q = 0
COL_CH = COL_Q + ATTN_WIDTH
COL_CB = COL_CH + CONV_WIDTH
COL_CC = COL_CB + CONV_WIDTH
COL_K = COL_CC + CONV_WIDTH
COL_V = COL_K + KV_WIDTH
PROJ_A_WIDTH = COL_V + KV_WIDTH
COL_GL = 0
COL_MQ = COL_GL + GATE_WIDTH
PROJ_B_WIDTH = COL_MQ + MEM_WIDTH

MXU_DT = jnp.bfloat16
GATE_DT = jnp.bfloat16
VMEM_LIMIT = 56 * 1024 * 1024

PROJ_TM = 1024
PROJ_A_TN = PROJ_A_WIDTH // 3
PROJ_B_TN = PROJ_B_WIDTH // 4
ATTN_BLOCKS = 4
MIX_T = 256
ROUTE_T = 256
ROUTE_LANES = 128
PEER_C = 512
PEER_EB = 256
PEER_SUB = 8
LN2_T = 512

F32 = jnp.float32


def _cparams(*sem):
    return pltpu.CompilerParams(dimension_semantics=sem, vmem_limit_bytes=VMEM_LIMIT)


def _resident(shape):
    nd = len(shape)
    return pl.BlockSpec(shape, lambda *_: (0,) * nd, pipeline_mode=pl.Buffered(1))


def _layer_norm(x, g, b):
    mu = jnp.mean(x, axis=-1, keepdims=True)
    xc = x - mu
    var = jnp.mean(xc * xc, axis=-1, keepdims=True)
    return xc * lax.rsqrt(var + LN_EPS) * g + b


def _mm_kernel(x_ref, w_ref, o_ref, xb_ref):
    @pl.when(pl.program_id(1) == 0)
    def _():
        xb_ref[...] = x_ref[...].astype(xb_ref.dtype)

    o_ref[...] = jnp.dot(xb_ref[...], w_ref[...], preferred_element_type=F32).astype(o_ref.dtype)


def _matmul(x, w, out_dtype, tm, tn):
    m, k = x.shape
    n = w.shape[1]
    tm, tn = min(tm, m), min(tn, n)
    assert m % tm == 0 and n % tn == 0
    return pl.pallas_call(
        _mm_kernel,
        grid=(m // tm, n // tn),
        in_specs=[pl.BlockSpec((tm, k), lambda i, j: (i, 0)),
                  pl.BlockSpec((k, tn), lambda i, j: (0, j))],
        out_specs=pl.BlockSpec((tm, tn), lambda i, j: (i, j)),
        out_shape=jax.ShapeDtypeStruct((m, n), out_dtype),
        scratch_shapes=[pltpu.VMEM((tm, k), w.dtype)],
        compiler_params=_cparams("parallel", "arbitrary"),
        name="matmul",
    )(x, w)


def _attn_kernel(sink_ref, q_ref, kp_ref, km_ref, kn_ref, vp_ref, vm_ref, vn_ref, o_ref, *, tiles_per_seq):
    t = pl.program_id(0)
    pos = t % tiles_per_seq
    lo = jnp.where(pos == 0, WINDOW, 0)
    hi = jnp.where(pos == tiles_per_seq - 1, 2 * WINDOW, 3 * WINDOW)
    kall = jnp.concatenate([kp_ref[...], km_ref[...], kn_ref[...]], axis=0)
    vall = jnp.concatenate([vp_ref[...], vm_ref[...], vn_ref[...]], axis=0)
    qi = lax.broadcasted_iota(jnp.int32, (WINDOW, 3 * WINDOW), 0)
    si = lax.broadcasted_iota(jnp.int32, (WINDOW, 3 * WINDOW), 1)
    dist = jnp.abs(qi + WINDOW - si)
    band = dist <= WINDOW
    distf = dist.astype(F32)
    scale = 1.0 / math.sqrt(HEAD_DIM)
    n_blocks = q_ref.shape[0] // WINDOW
    for r in range(n_blocks):
        valid = band
        if r == 0:
            valid = valid & (si >= lo)
        if r == n_blocks - 1:
            valid = valid & (si < hi)
        rows = slice(r * WINDOW, (r + 1) * WINDOW)
        for kv in range(N_KV_HEADS):
            cols = slice(kv * HEAD_DIM, (kv + 1) * HEAD_DIM)
            kc = kall[r * WINDOW:(r + 3) * WINDOW, cols]
            vc = vall[r * WINDOW:(r + 3) * WINDOW, cols]
            heads = [kv * GQA_GROUP + g for g in range(GQA_GROUP)]
            q4 = jnp.concatenate([q_ref[rows, h * HEAD_DIM:(h + 1) * HEAD_DIM] for h in heads], axis=0)
            s4 = lax.dot_general(q4, kc, (((1,), (1,)), ((), ())), preferred_element_type=F32) * scale
            ps, inv = [], []
            for g, h in enumerate(heads):
                slope = 2.0 ** (-8.0 * (h + 1) / N_Q_HEADS)
                s = jnp.where(valid, s4[g * WINDOW:(g + 1) * WINDOW] - slope * distf, -jnp.inf)
                sk = sink_ref[h]
                m = jnp.maximum(jnp.max(s, axis=-1, keepdims=True), sk)
                p = jnp.exp(s - m)
                denom = jnp.sum(p, axis=-1, keepdims=True) + jnp.exp(sk - m)
                ps.append(p.astype(vc.dtype))
                inv.append(1.0 / denom)
            o4 = jnp.dot(jnp.concatenate(ps, axis=0), vc, preferred_element_type=F32)
            for g, h in enumerate(heads):
                o_ref[rows, h * HEAD_DIM:(h + 1) * HEAD_DIM] = (
                    o4[g * WINDOW:(g + 1) * WINDOW] * inv[g]).astype(o_ref.dtype)


def _attention(proj, sink, seq_len):
    n = proj.shape[0]
    ta = ATTN_BLOCKS * WINDOW
    assert seq_len % ta == 0 and n % seq_len == 0
    nblk = n // WINDOW
    kcol, vcol = COL_K // KV_WIDTH, COL_V // KV_WIDTH

    def halo(col, side):
        if side < 0:
            return pl.BlockSpec((WINDOW, KV_WIDTH), lambda t: (jnp.maximum(t * ATTN_BLOCKS - 1, 0), col))
        return pl.BlockSpec((WINDOW, KV_WIDTH), lambda t: (jnp.minimum((t + 1) * ATTN_BLOCKS, nblk - 1), col))

    def main(col):
        return pl.BlockSpec((ta, KV_WIDTH), lambda t: (t, col))

    return pl.pallas_call(
        functools.partial(_attn_kernel, tiles_per_seq=seq_len // ta),
        grid=(n // ta,),
        in_specs=[pl.BlockSpec(memory_space=pltpu.SMEM),
                  pl.BlockSpec((ta, ATTN_WIDTH), lambda t: (t, COL_Q // ATTN_WIDTH)),
                  halo(kcol, -1), main(kcol), halo(kcol, 1),
                  halo(vcol, -1), main(vcol), halo(vcol, 1)],
        out_specs=pl.BlockSpec((ta, ATTN_WIDTH), lambda t: (t, 0)),
        out_shape=jax.ShapeDtypeStruct((n, ATTN_WIDTH), proj.dtype),
        compiler_params=_cparams("parallel"),
        name="attention",
    )(sink, proj, proj, proj, proj, proj, proj, proj)


def _mixer_kernel(attn_ref, ch_ref, cb_ref, cc_ref, chp_ref, ccp_ref, chn_ref, ccn_ref,
                  g0_ref, g1_ref, g2_ref, mq_ref, mk_ref, mv_ref, convw_ref,
                  wa_ref, wc_ref, wm_ref, o_ref, *, tiles_per_seq):
    t = pl.program_id(0)
    pos = t % tiles_per_seq
    tt = ch_ref.shape[0]
    z = cc_ref[...].astype(F32) * ch_ref[...].astype(F32)
    zp = ccp_ref[7:8, :].astype(F32) * chp_ref[7:8, :].astype(F32)
    zn = ccn_ref[0:1, :].astype(F32) * chn_ref[0:1, :].astype(F32)
    zp = jnp.where(pos == 0, 0.0, zp)
    zn = jnp.where(pos == tiles_per_seq - 1, 0.0, zn)
    row = lax.broadcasted_iota(jnp.int32, z.shape, 0)
    z_up = jnp.where(row == 0, zp, pltpu.roll(z, 1, 0))
    z_dn = jnp.where(row == tt - 1, zn, pltpu.roll(z, tt - 1, 0))
    conv = z_up * convw_ref[0:1, :] + z * convw_ref[1:2, :] + z_dn * convw_ref[2:3, :]
    cv = (cb_ref[...].astype(F32) * conv).astype(wc_ref.dtype)
    mscale = 1.0 / math.sqrt(MEM_HEAD_DIM)
    mo = []
    for hm in range(MEM_HEADS):
        cols = slice(hm * MEM_HEAD_DIM, (hm + 1) * MEM_HEAD_DIM)
        s = lax.dot_general(mq_ref[:, cols], mk_ref[:, cols], (((1,), (1,)), ((), ())),
                            preferred_element_type=F32) * mscale
        p = jnp.exp(s - jnp.max(s, axis=-1, keepdims=True))
        inv = 1.0 / jnp.sum(p, axis=-1, keepdims=True)
        o = jnp.dot(p.astype(mv_ref.dtype), mv_ref[:, cols], preferred_element_type=F32) * inv
        mo.append(o.astype(wm_ref.dtype))
    memo = jnp.concatenate(mo, axis=-1)
    a_out = jnp.dot(attn_ref[...], wa_ref[...], preferred_element_type=F32)
    c_out = jnp.dot(cv, wc_ref[...], preferred_element_type=F32)
    m_out = jnp.dot(memo, wm_ref[...], preferred_element_type=F32)
    merged = (jax.nn.sigmoid(g0_ref[...].astype(F32)) * a_out
              + jax.nn.sigmoid(g1_ref[...].astype(F32)) * c_out
              + jax.nn.sigmoid(g2_ref[...].astype(F32)) * m_out)
    o_ref[...] = merged.astype(o_ref.dtype)


def _mixer(attn, proj, proj_b, mem_k, mem_v, conv_w, wa, wc, wm, seq_len):
    n = attn.shape[0]
    tt = MIX_T
    assert seq_len % tt == 0
    tiles_per_seq = seq_len // tt
    n8 = n // 8

    def col(c, width):
        return pl.BlockSpec((tt, width), lambda t: (t, c // width))

    def prev8(c):
        return pl.BlockSpec((8, CONV_WIDTH), lambda t: (jnp.maximum(t * (tt // 8) - 1, 0), c // CONV_WIDTH))

    def next8(c):
        return pl.BlockSpec((8, CONV_WIDTH), lambda t: (jnp.minimum((t + 1) * (tt // 8), n8 - 1), c // CONV_WIDTH))

    def mem_spec():
        return pl.BlockSpec((N_MEM, MEM_WIDTH), lambda t: (t // tiles_per_seq, 0))

    return pl.pallas_call(
        functools.partial(_mixer_kernel, tiles_per_seq=tiles_per_seq),
        grid=(n // tt,),
        in_specs=[pl.BlockSpec((tt, ATTN_WIDTH), lambda t: (t, 0)),
                  col(COL_CH, CONV_WIDTH), col(COL_CB, CONV_WIDTH), col(COL_CC, CONV_WIDTH),
                  prev8(COL_CH), prev8(COL_CC), next8(COL_CH), next8(COL_CC),
                  col(COL_GL, D_MODEL), col(COL_GL + D_MODEL, D_MODEL), col(COL_GL + 2 * D_MODEL, D_MODEL),
                  col(COL_MQ, MEM_WIDTH), mem_spec(), mem_spec(),
                  _resident(conv_w.shape), _resident(wa.shape), _resident(wc.shape), _resident(wm.shape)],
        out_specs=pl.BlockSpec((tt, D_MODEL), lambda t: (t, 0)),
        out_shape=jax.ShapeDtypeStruct((n, D_MODEL), attn.dtype),
        compiler_params=_cparams("parallel"),
        name="mixer",
    )(attn, proj, proj, proj, proj, proj, proj, proj, proj_b, proj_b, proj_b, proj_b,
      mem_k, mem_v, conv_w, wa, wc, wm)


def _sort_network(n):
    pairs, p = [], 1
    while p < n:
        k = p
        while k >= 1:
            for j in range(k % p, n - k, 2 * k):
                for i in range(min(k, n - j - k)):
                    if (i + j) // (2 * p) == (i + j + k) // (2 * p):
                        pairs.append((i + j, i + j + k))
            k //= 2
        p *= 2
    return pairs


def _compare_exchange(v, i, j):
    v[i], v[j] = jnp.maximum(v[i], v[j]), jnp.minimum(v[i], v[j])


def _top16_rows(v, n_valid):
    assert 2 * n_valid >= PEER_TOPK
    v = list(v)
    for i, j in _sort_network(PEER_TOPK):
        if j < n_valid:
            _compare_exchange(v, i, j)
    for shift in (4, 2, 1):
        other = [pltpu.roll(x, shift, 0) for x in v[:n_valid]]
        merged = []
        for a in range(PEER_TOPK):
            b = PEER_TOPK - 1 - a
            if a < n_valid and b < n_valid:
                merged.append(jnp.maximum(v[a], other[b]))
            else:
                merged.append(v[a] if a < n_valid else other[b])
        v, n_valid = merged, PEER_TOPK
        for d in (8, 4, 2, 1):
            for a in range(PEER_TOPK):
                if a & d == 0:
                    _compare_exchange(v, a, a + d)
    return v


def _rows_to_block(rows, sub):
    out = rows[0]
    for k in range(1, 8):
        out = jnp.where(sub == k, rows[k], out)
    return out


def _all_sublanes(x, op):
    for shift in (4, 2, 1):
        x = op(x, pltpu.roll(x, shift, 0))
    return x


def _route_column(s0, s1):
    n_slab = N_KEYS // 8
    slabs0 = [s0[8 * v:8 * v + 8] for v in range(n_slab)]
    slabs1 = [s1[8 * v:8 * v + 8] for v in range(n_slab)]
    t0 = _top16_rows(slabs0, n_slab)
    t1 = _top16_rows(slabs1, n_slab)
    sub = lax.broadcasted_iota(jnp.int32, slabs0[0].shape, 0)
    v0 = (_rows_to_block(t0[:8], sub), _rows_to_block(t0[8:], sub))
    v1_lo = _rows_to_block(t1[:8], sub)
    cand = [t0[0] + v1_lo, t0[0] + _rows_to_block(t1[8:], sub), t0[1] + v1_lo]
    for a in range(2, 8):
        cand.append(jnp.where(sub < PEER_TOPK // (a + 1), t0[a] + v1_lo, -jnp.inf))
    cand.append(v0[1] + t1[0])
    thr = _top16_rows(cand + [None] * (PEER_TOPK - len(cand)), len(cand))[PEER_TOPK - 1]
    e0 = [jnp.exp(x - t0[0]) for x in v0]
    zacc = [jnp.zeros_like(thr), jnp.zeros_like(thr)]
    lowest = []
    for b in range(PEER_TOPK):
        e1b = jnp.exp(t1[b] - t1[0])
        low = None
        for half in range(2):
            sel = (v0[half] + t1[b]) >= thr
            zacc[half] = zacc[half] + jnp.where(sel, e1b, 0.0)
            cur = jnp.where(sel, v0[half], jnp.inf)
            low = cur if low is None else jnp.minimum(low, cur)
        lowest.append(_all_sublanes(low, jnp.minimum))
    inv_z = 1.0 / _all_sublanes(e0[0] * zacc[0] + e0[1] * zacc[1], jnp.add)
    c0, m0, r1, m1 = [], [], [], []
    for v in range(n_slab):
        c = jnp.zeros_like(slabs0[v])
        for b in range(PEER_TOPK):
            c = jnp.where(slabs0[v] >= lowest[b], float(b + 1), c)
        r = jnp.full_like(slabs1[v], float(PEER_TOPK))
        for a in reversed(range(PEER_TOPK)):
            r = jnp.where(slabs1[v] >= t1[a], float(a), r)
        c0.append(c)
        r1.append(r)
        m0.append(jnp.exp(slabs0[v] - t0[0]) * inv_z)
        m1.append(jnp.exp(slabs1[v] - t1[0]))
    cat = lambda xs: jnp.concatenate(xs, axis=0)
    return cat(c0), cat(m0), cat(r1), cat(m1)


COUNT_MASK = 0xFF


def _pack_count_factor(count, factor):
    assert jnp.dtype(GATE_DT).itemsize == 2
    top = lax.bitcast_convert_type(factor.astype(GATE_DT).astype(F32), jnp.uint32)
    return top | count.astype(jnp.int32).astype(jnp.uint32)


def _ln1_route_kernel(x_ref, mg_ref, wo_ref, g_ref, b_ref, wq_ref, keys_ref,
                      h_ref, ht_ref, cm_ref, r1_ref, m1_ref):
    mix = jnp.dot(mg_ref[...], wo_ref[...], preferred_element_type=F32)
    h = _layer_norm(ALPHA * x_ref[...] + mix, g_ref[...], b_ref[...])
    h_ref[...] = h
    ht_ref[...] = h.T.astype(ht_ref.dtype)
    q_rows = 2 * PEER_HALF

    def project(hd):
        rows = pl.ds(pl.multiple_of(hd * q_rows, q_rows), q_rows)
        return jnp.dot(wq_ref[rows, :], ht_ref[...], preferred_element_type=F32).astype(MXU_DT)

    def head(hd, qt):
        out_rows = pl.ds(pl.multiple_of(hd * N_KEYS, N_KEYS), N_KEYS)
        nxt = None
        for col in range(x_ref.shape[0] // ROUTE_LANES):
            lanes = slice(col * ROUTE_LANES, (col + 1) * ROUTE_LANES)
            s0 = jnp.dot(keys_ref[2 * hd], qt[:PEER_HALF, lanes], preferred_element_type=F32)
            s1 = jnp.dot(keys_ref[2 * hd + 1], qt[PEER_HALF:, lanes], preferred_element_type=F32)
            if col == 0:
                nxt = project(jnp.minimum(hd + 1, PEER_HEADS - 1))
            c0, m0, r1, m1 = _route_column(s0, s1)
            cm_ref[out_rows, lanes] = _pack_count_factor(c0, m0)
            r1_ref[out_rows, lanes] = r1.astype(r1_ref.dtype)
            m1_ref[out_rows, lanes] = m1.astype(m1_ref.dtype)
        return nxt

    lax.fori_loop(0, PEER_HEADS, head, project(0))


def _ln1_route(x, merged, w_out, ln_g, ln_b, wq_t, keys):
    n = x.shape[0]
    tt = ROUTE_T
    rows = PEER_HEADS * N_KEYS
    tok = lambda t: (t, 0)
    tok_t = lambda t: (0, t)
    return pl.pallas_call(
        _ln1_route_kernel,
        grid=(n // tt,),
        in_specs=[pl.BlockSpec((tt, D_MODEL), tok), pl.BlockSpec((tt, D_MODEL), tok),
                  _resident(w_out.shape), _resident(ln_g.shape), _resident(ln_b.shape),
                  _resident(wq_t.shape), _resident(keys.shape)],
        out_specs=[pl.BlockSpec((tt, D_MODEL), tok), pl.BlockSpec((D_MODEL, tt), tok_t),
                   pl.BlockSpec((rows, tt), tok_t), pl.BlockSpec((rows, tt), tok_t),
                   pl.BlockSpec((rows, tt), tok_t)],
        out_shape=[jax.ShapeDtypeStruct((n, D_MODEL), F32),
                   jax.ShapeDtypeStruct((D_MODEL, n), MXU_DT),
                   jax.ShapeDtypeStruct((rows, n), jnp.uint32),
                   jax.ShapeDtypeStruct((rows, n), GATE_DT), jax.ShapeDtypeStruct((rows, n), GATE_DT)],
        compiler_params=_cparams("parallel"),
        name="ln1_route",
    )(x, merged, w_out, ln_g, ln_b, wq_t, keys)


def _peer_gate_block(block, pre, cm_ref, r1_ref, m1_ref):
    act = (0.5 * pre * (1.0 + lax.erf(pre * math.sqrt(0.5)))).astype(GATE_DT)
    per_block = pre.shape[0] // N_KEYS
    out = []
    for ii in range(per_block):
        i = block * per_block + ii
        gate = None
        for hd in range(PEER_HEADS):
            word = cm_ref[pl.ds(hd * N_KEYS + i, 1), :]
            m0 = lax.bitcast_convert_type(word, F32).astype(GATE_DT)
            c0 = (word & COUNT_MASK).astype(jnp.int32).astype(F32).astype(GATE_DT)
            r1 = r1_ref[hd * N_KEYS:(hd + 1) * N_KEYS, :]
            m1 = m1_ref[hd * N_KEYS:(hd + 1) * N_KEYS, :]
            term = jnp.where(r1 < c0, m1, jnp.zeros_like(m1)) * m0
            gate = term if gate is None else gate + term
        out.append((gate * act[ii * N_KEYS:(ii + 1) * N_KEYS, :]).astype(MXU_DT))
    return jnp.concatenate(out, axis=0)


def _peer_kernel(ht_ref, u_ref, vt_ref, cm_ref, r1_ref, m1_ref, o_ref):
    g = pl.program_id(1)
    eb = PEER_EB
    n_sub = u_ref.shape[0] // eb

    @pl.when(g == 0)
    def _():
        o_ref[...] = jnp.zeros_like(o_ref)

    routing = (cm_ref, r1_ref, m1_ref)
    pre = [jnp.dot(u_ref[k * eb:(k + 1) * eb, :], ht_ref[...], preferred_element_type=F32)
           for k in range(n_sub)]
    for k in range(n_sub):
        w = _peer_gate_block(g * n_sub + k, pre[k], *routing)
        o_ref[...] += jnp.dot(vt_ref[:, k * eb:(k + 1) * eb], w, preferred_element_type=F32)


def _peer_dense(ht, u, vt, cm, r1, m1):
    n = ht.shape[1]
    c, step = PEER_C, PEER_EB * PEER_SUB
    rows = PEER_HEADS * N_KEYS
    tok_t = lambda t, g: (0, t)
    tile = lambda r: pl.BlockSpec((r, c), tok_t)
    return pl.pallas_call(
        _peer_kernel,
        grid=(n // c, N_EXPERTS // step),
        in_specs=[tile(D_MODEL),
                  pl.BlockSpec((step, D_MODEL), lambda t, g: (g, 0)),
                  pl.BlockSpec((D_MODEL, step), lambda t, g: (0, g)),
                  tile(rows), tile(rows), tile(rows)],
        out_specs=pl.BlockSpec((D_MODEL, c), tok_t, pipeline_mode=pl.Buffered(1)),
        out_shape=jax.ShapeDtypeStruct((D_MODEL, n), F32),
        compiler_params=_cparams("parallel", "arbitrary"),
        name="peer_dense",
    )(ht, u, vt, cm, r1, m1)


def _ln2_kernel(h_ref, pt_ref, g_ref, b_ref, o_ref):
    o_ref[...] = _layer_norm(ALPHA * h_ref[...] + pt_ref[...].T, g_ref[...], b_ref[...])


def _ln2(h, peer_t, ln_g, ln_b):
    n = h.shape[0]
    tt = LN2_T
    return pl.pallas_call(
        _ln2_kernel,
        grid=(n // tt,),
        in_specs=[pl.BlockSpec((tt, D_MODEL), lambda t: (t, 0)),
                  pl.BlockSpec((D_MODEL, tt), lambda t: (0, t)),
                  _resident(ln_g.shape), _resident(ln_b.shape)],
        out_specs=pl.BlockSpec((tt, D_MODEL), lambda t: (t, 0)),
        out_shape=jax.ShapeDtypeStruct((n, D_MODEL), F32),
        compiler_params=_cparams("parallel"),
        name="ln2",
    )(h, peer_t, ln_g, ln_b)


def _prepare(w_in, sink, conv_w, w_mem_k, w_mem_v, w_attn_o, w_conv_out, w_mem_o, w_out,
             ln1_g, ln1_b, w_peer_q, peer_keys, peer_u, peer_v, ln2_g, ln2_b):
    cuts, off = {}, 0
    for name, width in (("q", ATTN_WIDTH), ("k", KV_WIDTH), ("v", KV_WIDTH), ("ch", CONV_WIDTH),
                        ("cb", CONV_WIDTH), ("cc", CONV_WIDTH), ("mq", MEM_WIDTH), ("gl", GATE_WIDTH)):
        cuts[name] = w_in[:, off:off + width]
        off += width
    w_a = jnp.concatenate([cuts[k] for k in ("q", "ch", "cb", "cc", "k", "v")], axis=1)
    w_b = jnp.concatenate([cuts[k] for k in ("gl", "mq")], axis=1)
    row = lambda v: v.reshape(1, -1).astype(F32)
    return dict(
        w_a=w_a.astype(MXU_DT), w_b=w_b.astype(MXU_DT), sink=sink.astype(F32), conv_w=conv_w.astype(F32),
        w_mem_k=w_mem_k.astype(MXU_DT), w_mem_v=w_mem_v.astype(MXU_DT),
        wa=w_attn_o.astype(MXU_DT), wc=w_conv_out.astype(MXU_DT), wm=w_mem_o.astype(MXU_DT),
        w_out=w_out.astype(MXU_DT), ln1_g=row(ln1_g), ln1_b=row(ln1_b),
        wq_t=w_peer_q.T.astype(MXU_DT),
        keys=peer_keys.reshape(PEER_HEADS * 2, N_KEYS, PEER_HALF).astype(MXU_DT),
        u=peer_u.astype(MXU_DT), vt=peer_v.T.astype(MXU_DT), ln2_g=row(ln2_g), ln2_b=row(ln2_b))


def _encode_group(x, mem, p):
    b, s, d = x.shape
    x2 = x.reshape(b * s, d)
    mem2 = mem.reshape(b * N_MEM, d)
    proj = _matmul(x2, p["w_a"], MXU_DT, PROJ_TM, PROJ_A_TN)
    proj_b = _matmul(x2, p["w_b"], MXU_DT, PROJ_TM, PROJ_B_TN)
    mem_k = _matmul(mem2, p["w_mem_k"], MXU_DT, 512, 512)
    mem_v = _matmul(mem2, p["w_mem_v"], MXU_DT, 512, 512)
    attn = _attention(proj, p["sink"], s)
    merged = _mixer(attn, proj, proj_b, mem_k, mem_v, p["conv_w"], p["wa"], p["wc"], p["wm"], s)
    h, ht, cm, r1, m1 = _ln1_route(x2, merged, p["w_out"], p["ln1_g"], p["ln1_b"], p["wq_t"], p["keys"])
    peer_t = _peer_dense(ht, p["u"], p["vt"], cm, r1, m1)
    return _ln2(h, peer_t, p["ln2_g"], p["ln2_b"]).reshape(b, s, d)


def kernel(x_prompt, x_sample, mem_prompt, mem_sample, w_in, sink, conv_w, w_mem_k, w_mem_v, w_attn_o,
           w_conv_out, w_mem_o, w_out, ln1_g, ln1_b, w_peer_q, peer_keys, peer_u, peer_v, ln2_g, ln2_b):
    assert w_in.shape[0] == DEPTH == 1
    p = _prepare(w_in[0], sink[0], conv_w[0], w_mem_k[0], w_mem_v[0], w_attn_o[0], w_conv_out[0],
                 w_mem_o[0], w_out[0], ln1_g[0], ln1_b[0], w_peer_q[0], peer_keys[0], peer_u[0],
                 peer_v[0], ln2_g[0], ln2_b[0])
    return (_encode_group(x_prompt, mem_prompt, p), _encode_group(x_sample, mem_sample, p))
```

```python
import functools
import math

import jax
import jax.numpy as jnp
from jax import lax
from jax.experimental import pallas as pl
from jax.experimental.pallas import tpu as pltpu

D_MODEL = 2048
N_Q_HEADS = 8
N_KV_HEADS = 2
GQA_GROUP = N_Q_HEADS // N_KV_HEADS
HEAD_DIM = 128
WINDOW = 128
ATTN_WIDTH = N_Q_HEADS * HEAD_DIM
KV_WIDTH = N_KV_HEADS * HEAD_DIM
CONV_WIDTH = 1024
N_MEM = 256
MEM_HEADS = 4
MEM_HEAD_DIM = 256
MEM_WIDTH = MEM_HEADS * MEM_HEAD_DIM
PEER_HEADS = 8
N_KEYS = 128
N_EXPERTS = N_KEYS * N_KEYS
PEER_HALF = 128
PEER_TOPK = 16
LN_EPS = 1e-5
DEPTH = 1
ALPHA = (2.0 * DEPTH) ** 0.25
GATE_WIDTH = 3 * D_MODEL
IN_WIDTH = ATTN_WIDTH + 2 * KV_WIDTH + 3 * CONV_WIDTH + MEM_WIDTH + GATE_WIDTH

COL_Q = 0
COL_CH = COL_Q + ATTN_WIDTH
COL_CB = COL_CH + CONV_WIDTH
COL_CC = COL_CB + CONV_WIDTH
COL_K = COL_CC + CONV_WIDTH
COL_V = COL_K + KV_WIDTH
PROJ_A_WIDTH = COL_V + KV_WIDTH
COL_GL = 0
COL_MQ = COL_GL + GATE_WIDTH
PROJ_B_WIDTH = COL_MQ + MEM_WIDTH

MXU_DT = jnp.bfloat16
GATE_DT = jnp.bfloat16
VMEM_LIMIT = 56 * 1024 * 1024

PROJ_TM = 1024
PROJ_A_TN = PROJ_A_WIDTH // 3
PROJ_B_TN = PROJ_B_WIDTH // 4
ATTN_BLOCKS = 4
MIX_T = 512
ROUTE_T = 256
ROUTE_LANES = 128
PEER_C = 512
PEER_EB = 256
PEER_SUB = 8
LN2_T = 512

F32 = jnp.float32


def _cparams(*sem):
    return pltpu.CompilerParams(dimension_semantics=sem, vmem_limit_bytes=VMEM_LIMIT)


def _resident(shape):
    nd = len(shape)
    return pl.BlockSpec(shape, lambda *_: (0,) * nd, pipeline_mode=pl.Buffered(1))


def _layer_norm(x, g, b):
    mu = jnp.mean(x, axis=-1, keepdims=True)
    xc = x - mu
    var = jnp.mean(xc * xc, axis=-1, keepdims=True)
    return xc * lax.rsqrt(var + LN_EPS) * g + b


def _mm_kernel(x_ref, w_ref, o_ref, xb_ref):
    @pl.when(pl.program_id(1) == 0)
    def _():
        xb_ref[...] = x_ref[...].astype(xb_ref.dtype)

    o_ref[...] = jnp.dot(xb_ref[...], w_ref[...], preferred_element_type=F32).astype(o_ref.dtype)


def _matmul(x, w, out_dtype, tm, tn):
    m, k = x.shape
    n = w.shape[1]
    tm, tn = min(tm, m), min(tn, n)
    assert m % tm == 0 and n % tn == 0
    return pl.pallas_call(
        _mm_kernel,
        grid=(m // tm, n // tn),
        in_specs=[pl.BlockSpec((tm, k), lambda i, j: (i, 0)),
                  pl.BlockSpec((k, tn), lambda i, j: (0, j))],
        out_specs=pl.BlockSpec((tm, tn), lambda i, j: (i, j)),
        out_shape=jax.ShapeDtypeStruct((m, n), out_dtype),
        scratch_shapes=[pltpu.VMEM((tm, k), w.dtype)],
        compiler_params=_cparams("parallel", "arbitrary"),
        name="matmul",
    )(x, w)


def _attn_kernel(sink_ref, q_ref, kp_ref, km_ref, kn_ref, vp_ref, vm_ref, vn_ref, o_ref, *, tiles_per_seq):
    t = pl.program_id(0)
    pos = t % tiles_per_seq
    lo = jnp.where(pos == 0, WINDOW, 0)
    hi = jnp.where(pos == tiles_per_seq - 1, 2 * WINDOW, 3 * WINDOW)
    kall = jnp.concatenate([kp_ref[...], km_ref[...], kn_ref[...]], axis=0)
    vall = jnp.concatenate([vp_ref[...], vm_ref[...], vn_ref[...]], axis=0)
    qi = lax.broadcasted_iota(jnp.int32, (WINDOW, 3 * WINDOW), 0)
    si = lax.broadcasted_iota(jnp.int32, (WINDOW, 3 * WINDOW), 1)
    dist = jnp.abs(qi + WINDOW - si)
    band = dist <= WINDOW
    distf = dist.astype(F32)
    scale = 1.0 / math.sqrt(HEAD_DIM)
    n_blocks = q_ref.shape[0] // WINDOW
    for r in range(n_blocks):
        valid = band
        if r == 0:
            valid = valid & (si >= lo)
        if r == n_blocks - 1:
            valid = valid & (si < hi)
        rows = slice(r * WINDOW, (r + 1) * WINDOW)
        for kv in range(N_KV_HEADS):
            cols = slice(kv * HEAD_DIM, (kv + 1) * HEAD_DIM)
            kc = kall[r * WINDOW:(r + 3) * WINDOW, cols]
            vc = vall[r * WINDOW:(r + 3) * WINDOW, cols]
            heads = [kv * GQA_GROUP + g for g in range(GQA_GROUP)]
            q4 = jnp.concatenate([q_ref[rows, h * HEAD_DIM:(h + 1) * HEAD_DIM] for h in heads], axis=0)
            s4 = lax.dot_general(q4, kc, (((1,), (1,)), ((), ())), preferred_element_type=F32) * scale
            ps, inv = [], []
            for g, h in enumerate(heads):
                slope = 2.0 ** (-8.0 * (h + 1) / N_Q_HEADS)
                s = jnp.where(valid, s4[g * WINDOW:(g + 1) * WINDOW] - slope * distf, -jnp.inf)
                sk = sink_ref[h]
                m = jnp.maximum(jnp.max(s, axis=-1, keepdims=True), sk)
                p = jnp.exp(s - m)
                denom = jnp.sum(p, axis=-1, keepdims=True) + jnp.exp(sk - m)
                ps.append(p.astype(vc.dtype))
                inv.append(1.0 / denom)
            o4 = jnp.dot(jnp.concatenate(ps, axis=0), vc, preferred_element_type=F32)
            for g, h in enumerate(heads):
                o_ref[rows, h * HEAD_DIM:(h + 1) * HEAD_DIM] = (
                    o4[g * WINDOW:(g + 1) * WINDOW] * inv[g]).astype(o_ref.dtype)


def _attention(proj, sink, seq_len):
    n = proj.shape[0]
    ta = ATTN_BLOCKS * WINDOW
    assert seq_len % ta == 0 and n % seq_len == 0
    nblk = n // WINDOW
    kcol, vcol = COL_K // KV_WIDTH, COL_V // KV_WIDTH

    def halo(col, side):
        if side < 0:
            return pl.BlockSpec((WINDOW, KV_WIDTH), lambda t: (jnp.maximum(t * ATTN_BLOCKS - 1, 0), col))
        return pl.BlockSpec((WINDOW, KV_WIDTH), lambda t: (jnp.minimum((t + 1) * ATTN_BLOCKS, nblk - 1), col))

    def main(col):
        return pl.BlockSpec((ta, KV_WIDTH), lambda t: (t, col))

    return pl.pallas_call(
        functools.partial(_attn_kernel, tiles_per_seq=seq_len // ta),
        grid=(n // ta,),
        in_specs=[pl.BlockSpec(memory_space=pltpu.SMEM),
                  pl.BlockSpec((ta, ATTN_WIDTH), lambda t: (t, COL_Q // ATTN_WIDTH)),
                  halo(kcol, -1), main(kcol), halo(kcol, 1),
                  halo(vcol, -1), main(vcol), halo(vcol, 1)],
        out_specs=pl.BlockSpec((ta, ATTN_WIDTH), lambda t: (t, 0)),
        out_shape=jax.ShapeDtypeStruct((n, ATTN_WIDTH), proj.dtype),
        compiler_params=_cparams("parallel"),
        name="attention",
    )(sink, proj, proj, proj, proj, proj, proj, proj)


def _mixer_kernel(attn_ref, ch_ref, cb_ref, cc_ref, chp_ref, ccp_ref, chn_ref, ccn_ref,
                  g0_ref, g1_ref, g2_ref, mq_ref, mk_ref, mv_ref, convw_ref,
                  wa_ref, wc_ref, wm_ref, o_ref, *, tiles_per_seq):
    t = pl.program_id(0)
    pos = t % tiles_per_seq
    tt = ch_ref.shape[0]
    z = cc_ref[...].astype(F32) * ch_ref[...].astype(F32)
    zp = ccp_ref[7:8, :].astype(F32) * chp_ref[7:8, :].astype(F32)
    zn = ccn_ref[0:1, :].astype(F32) * chn_ref[0:1, :].astype(F32)
    zp = jnp.where(pos == 0, 0.0, zp)
    zn = jnp.where(pos == tiles_per_seq - 1, 0.0, zn)
    row = lax.broadcasted_iota(jnp.int32, z.shape, 0)
    z_up = jnp.where(row == 0, zp, pltpu.roll(z, 1, 0))
    z_dn = jnp.where(row == tt - 1, zn, pltpu.roll(z, tt - 1, 0))
    conv = z_up * convw_ref[0:1, :] + z * convw_ref[1:2, :] + z_dn * convw_ref[2:3, :]
    cv = (cb_ref[...].astype(F32) * conv).astype(wc_ref.dtype)
    mscale = 1.0 / math.sqrt(MEM_HEAD_DIM)
    mo = []
    for hm in range(MEM_HEADS):
        cols = slice(hm * MEM_HEAD_DIM, (hm + 1) * MEM_HEAD_DIM)
        s = lax.dot_general(mq_ref[:, cols], mk_ref[:, cols], (((1,), (1,)), ((), ())),
                            preferred_element_type=F32) * mscale
        p = jnp.exp(s - jnp.max(s, axis=-1, keepdims=True))
        inv = 1.0 / jnp.sum(p, axis=-1, keepdims=True)
        o = jnp.dot(p.astype(mv_ref.dtype), mv_ref[:, cols], preferred_element_type=F32) * inv
        mo.append(o.astype(wm_ref.dtype))
    memo = jnp.concatenate(mo, axis=-1)
    a_out = jnp.dot(attn_ref[...], wa_ref[...], preferred_element_type=F32)
    c_out = jnp.dot(cv, wc_ref[...], preferred_element_type=F32)
    m_out = jnp.dot(memo, wm_ref[...], preferred_element_type=F32)
    merged = (jax.nn.sigmoid(g0_ref[...].astype(F32)) * a_out
              + jax.nn.sigmoid(g1_ref[...].astype(F32)) * c_out
              + jax.nn.sigmoid(g2_ref[...].astype(F32)) * m_out)
    o_ref[...] = merged.astype(o_ref.dtype)


def _mixer(attn, proj, proj_b, mem_k, mem_v, conv_w, wa, wc, wm, seq_len):
    n = attn.shape[0]
    tt = MIX_T
    assert seq_len % tt == 0
    tiles_per_seq = seq_len // tt
    n8 = n // 8

    def col(c, width):
        return pl.BlockSpec((tt, width), lambda t: (t, c // width))

    def prev8(c):
        return pl.BlockSpec((8, CONV_WIDTH), lambda t: (jnp.maximum(t * (tt // 8) - 1, 0), c // CONV_WIDTH))

    def next8(c):
        return pl.BlockSpec((8, CONV_WIDTH), lambda t: (jnp.minimum((t + 1) * (tt // 8), n8 - 1), c // CONV_WIDTH))

    def mem_spec():
        return pl.BlockSpec((N_MEM, MEM_WIDTH), lambda t: (t // tiles_per_seq, 0))

    return pl.pallas_call(
        functools.partial(_mixer_kernel, tiles_per_seq=tiles_per_seq),
        grid=(n // tt,),
        in_specs=[pl.BlockSpec((tt, ATTN_WIDTH), lambda t: (t, 0)),
                  col(COL_CH, CONV_WIDTH), col(COL_CB, CONV_WIDTH), col(COL_CC, CONV_WIDTH),
                  prev8(COL_CH), prev8(COL_CC), next8(COL_CH), next8(COL_CC),
                  col(COL_GL, D_MODEL), col(COL_GL + D_MODEL, D_MODEL), col(COL_GL + 2 * D_MODEL, D_MODEL),
                  col(COL_MQ, MEM_WIDTH), mem_spec(), mem_spec(),
                  _resident(conv_w.shape), _resident(wa.shape), _resident(wc.shape), _resident(wm.shape)],
        out_specs=pl.BlockSpec((tt, D_MODEL), lambda t: (t, 0)),
        out_shape=jax.ShapeDtypeStruct((n, D_MODEL), attn.dtype),
        compiler_params=_cparams("parallel"),
        name="mixer",
    )(attn, proj, proj, proj, proj, proj, proj, proj, proj_b, proj_b, proj_b, proj_b,
      mem_k, mem_v, conv_w, wa, wc, wm)


def _sort_network(n):
    pairs, p = [], 1
    while p < n:
        k = p
        while k >= 1:
            for j in range(k % p, n - k, 2 * k):
                for i in range(min(k, n - j - k)):
                    if (i + j) // (2 * p) == (i + j + k) // (2 * p):
                        pairs.append((i + j, i + j + k))
            k //= 2
        p *= 2
    return pairs


def _compare_exchange(v, i, j):
    v[i], v[j] = jnp.maximum(v[i], v[j]), jnp.minimum(v[i], v[j])


def _top16_rows(v, n_valid):
    assert 2 * n_valid >= PEER_TOPK
    v = list(v)
    for i, j in _sort_network(PEER_TOPK):
        if j < n_valid:
            _compare_exchange(v, i, j)
    for shift in (4, 2, 1):
        other = [pltpu.roll(x, shift, 0) for x in v[:n_valid]]
        merged = []
        for a in range(PEER_TOPK):
            b = PEER_TOPK - 1 - a
            if a < n_valid and b < n_valid:
                merged.append(jnp.maximum(v[a], other[b]))
            else:
                merged.append(v[a] if a < n_valid else other[b])
        v, n_valid = merged, PEER_TOPK
        for d in (8, 4, 2, 1):
            for a in range(PEER_TOPK):
                if a & d == 0:
                    _compare_exchange(v, a, a + d)
    return v


def _rows_to_block(rows, sub):
    out = rows[0]
    for k in range(1, 8):
        out = jnp.where(sub == k, rows[k], out)
    return out


def _all_sublanes(x, op):
    for shift in (4, 2, 1):
        x = op(x, pltpu.roll(x, shift, 0))
    return x


def _route_column(s0, s1):
    n_slab = N_KEYS // 8
    slabs0 = [s0[8 * v:8 * v + 8] for v in range(n_slab)]
    slabs1 = [s1[8 * v:8 * v + 8] for v in range(n_slab)]
    t0 = _top16_rows(slabs0, n_slab)
    t1 = _top16_rows(slabs1, n_slab)
    sub = lax.broadcasted_iota(jnp.int32, slabs0[0].shape, 0)
    v0 = (_rows_to_block(t0[:8], sub), _rows_to_block(t0[8:], sub))
    v1_lo = _rows_to_block(t1[:8], sub)
    cand = [t0[0] + v1_lo, t0[0] + _rows_to_block(t1[8:], sub), t0[1] + v1_lo]
    for a in range(2, 8):
        cand.append(jnp.where(sub < PEER_TOPK // (a + 1), t0[a] + v1_lo, -jnp.inf))
    cand.append(v0[1] + t1[0])
    thr = _top16_rows(cand + [None] * (PEER_TOPK - len(cand)), len(cand))[PEER_TOPK - 1]
    e0 = [jnp.exp(x - t0[0]) for x in v0]
    zacc = [jnp.zeros_like(thr), jnp.zeros_like(thr)]
    lowest = []
    for b in range(PEER_TOPK):
        e1b = jnp.exp(t1[b] - t1[0])
        low = None
        for half in range(2):
            sel = (v0[half] + t1[b]) >= thr
            zacc[half] = zacc[half] + jnp.where(sel, e1b, 0.0)
            cur = jnp.where(sel, v0[half], jnp.inf)
            low = cur if low is None else jnp.minimum(low, cur)
        lowest.append(_all_sublanes(low, jnp.minimum))
    inv_z = 1.0 / _all_sublanes(e0[0] * zacc[0] + e0[1] * zacc[1], jnp.add)
    c0, m0, r1, m1 = [], [], [], []
    for v in range(n_slab):
        c = jnp.zeros_like(slabs0[v])
        for b in range(PEER_TOPK):
            c = jnp.where(slabs0[v] >= lowest[b], float(b + 1), c)
        r = jnp.full_like(slabs1[v], float(PEER_TOPK))
        for a in reversed(range(PEER_TOPK)):
            r = jnp.where(slabs1[v] >= t1[a], float(a), r)
        c0.append(c)
        r1.append(r)
        m0.append(jnp.exp(slabs0[v] - t0[0]) * inv_z)
        m1.append(jnp.exp(slabs1[v] - t1[0]))
    cat = lambda xs: jnp.concatenate(xs, axis=0)
    return cat(c0), cat(m0), cat(r1), cat(m1)


COUNT_MASK = 0xFF


def _pack_count_factor(count, factor):
    assert jnp.dtype(GATE_DT).itemsize == 2
    top = lax.bitcast_convert_type(factor.astype(GATE_DT).astype(F32), jnp.uint32)
    return top | count.astype(jnp.int32).astype(jnp.uint32)


def _ln1_route_kernel(x_ref, mg_ref, wo_ref, g_ref, b_ref, wq_ref, keys_ref,
                      h_ref, ht_ref, cm_ref, r1_ref, m1_ref, qt_scr):
    mix = jnp.dot(mg_ref[...], wo_ref[...], preferred_element_type=F32)
    h = _layer_norm(ALPHA * x_ref[...] + mix, g_ref[...], b_ref[...])
    h_ref[...] = h
    ht = h.T.astype(ht_ref.dtype)
    ht_ref[...] = ht
    qt_scr[...] = jnp.dot(wq_ref[...], ht, preferred_element_type=F32).astype(qt_scr.dtype)

    def route(hd, lanes):
        base = pl.multiple_of(hd * (2 * PEER_HALF), 2 * PEER_HALF)
        s0 = jnp.dot(keys_ref[2 * hd], qt_scr[pl.ds(base, PEER_HALF), lanes], preferred_element_type=F32)
        s1 = jnp.dot(keys_ref[2 * hd + 1], qt_scr[pl.ds(base + PEER_HALF, PEER_HALF), lanes],
                     preferred_element_type=F32)
        c0, m0, r1, m1 = _route_column(s0, s1)
        out_rows = pl.ds(pl.multiple_of(hd * N_KEYS, N_KEYS), N_KEYS)
        cm_ref[out_rows, lanes] = _pack_count_factor(c0, m0)
        r1_ref[out_rows, lanes] = r1.astype(r1_ref.dtype)
        m1_ref[out_rows, lanes] = m1.astype(m1_ref.dtype)

    def head(hd, carry):
        for col in range(x_ref.shape[0] // ROUTE_LANES):
            route(hd, slice(col * ROUTE_LANES, (col + 1) * ROUTE_LANES))
        return carry

    lax.fori_loop(0, PEER_HEADS, head, 0)


def _ln1_route(x, merged, w_out, ln_g, ln_b, wq_t, keys):
    n = x.shape[0]
    tt = ROUTE_T
    rows = PEER_HEADS * N_KEYS
    tok = lambda t: (t, 0)
    tok_t = lambda t: (0, t)
    return pl.pallas_call(
        _ln1_route_kernel,
        grid=(n // tt,),
        in_specs=[pl.BlockSpec((tt, D_MODEL), tok), pl.BlockSpec((tt, D_MODEL), tok),
                  _resident(w_out.shape), _resident(ln_g.shape), _resident(ln_b.shape),
                  _resident(wq_t.shape), _resident(keys.shape)],
        out_specs=[pl.BlockSpec((tt, D_MODEL), tok), pl.BlockSpec((D_MODEL, tt), tok_t),
                   pl.BlockSpec((rows, tt), tok_t), pl.BlockSpec((rows, tt), tok_t),
                   pl.BlockSpec((rows, tt), tok_t)],
        out_shape=[jax.ShapeDtypeStruct((n, D_MODEL), F32),
                   jax.ShapeDtypeStruct((D_MODEL, n), MXU_DT),
                   jax.ShapeDtypeStruct((rows, n), jnp.uint32),
                   jax.ShapeDtypeStruct((rows, n), GATE_DT), jax.ShapeDtypeStruct((rows, n), GATE_DT)],
        scratch_shapes=[pltpu.VMEM((PEER_HEADS * 2 * PEER_HALF, tt), MXU_DT)],
        compiler_params=_cparams("parallel"),
        name="ln1_route",
    )(x, merged, w_out, ln_g, ln_b, wq_t, keys)


def _peer_gate_block(block, pre, cm_ref, r1_ref, m1_ref):
    act = (0.5 * pre * (1.0 + lax.erf(pre * math.sqrt(0.5)))).astype(GATE_DT)
    per_block = pre.shape[0] // N_KEYS
    lanes = pre.shape[1]
    tiles = (N_KEYS // 16, 16, lanes)
    out = []
    for ii in range(per_block):
        i = block * per_block + ii
        gate = None
        for hd in range(PEER_HEADS):
            word = jnp.broadcast_to(cm_ref[pl.ds(hd * N_KEYS + i, 1), :], (16, lanes))
            m0 = lax.bitcast_convert_type(word, F32).astype(GATE_DT)
            c0 = (word & COUNT_MASK).astype(jnp.int32).astype(F32).astype(GATE_DT)
            r1 = r1_ref[hd * N_KEYS:(hd + 1) * N_KEYS, :].reshape(tiles)
            m1 = m1_ref[hd * N_KEYS:(hd + 1) * N_KEYS, :].reshape(tiles)
            term = jnp.where(r1 < c0[None], m1, jnp.zeros_like(m1)) * m0[None]
            gate = term if gate is None else gate + term
        gate = gate.reshape(N_KEYS, lanes)
        out.append((gate * act[ii * N_KEYS:(ii + 1) * N_KEYS, :]).astype(MXU_DT))
    return jnp.concatenate(out, axis=0)


def _peer_kernel(ht_ref, u_ref, vt_ref, cm_ref, r1_ref, m1_ref, o_ref):
    g = pl.program_id(1)
    eb = PEER_EB
    n_sub = u_ref.shape[0] // eb

    @pl.when(g == 0)
    def _():
        o_ref[...] = jnp.zeros_like(o_ref)

    routing = (cm_ref, r1_ref, m1_ref)
    pre = [jnp.dot(u_ref[k * eb:(k + 1) * eb, :], ht_ref[...], preferred_element_type=F32)
           for k in range(n_sub)]
    for k in range(n_sub):
        w = _peer_gate_block(g * n_sub + k, pre[k], *routing)
        o_ref[...] += jnp.dot(vt_ref[:, k * eb:(k + 1) * eb], w, preferred_element_type=F32)


def _peer_dense(ht, u, vt, cm, r1, m1):
    n = ht.shape[1]
    c, step = PEER_C, PEER_EB * PEER_SUB
    rows = PEER_HEADS * N_KEYS
    tok_t = lambda t, g: (0, t)
    tile = lambda r: pl.BlockSpec((r, c), tok_t)
    return pl.pallas_call(
        _peer_kernel,
        grid=(n // c, N_EXPERTS // step),
        in_specs=[tile(D_MODEL),
                  pl.BlockSpec((step, D_MODEL), lambda t, g: (g, 0)),
                  pl.BlockSpec((D_MODEL, step), lambda t, g: (0, g)),
                  tile(rows), tile(rows), tile(rows)],
        out_specs=pl.BlockSpec((D_MODEL, c), tok_t, pipeline_mode=pl.Buffered(1)),
        out_shape=jax.ShapeDtypeStruct((D_MODEL, n), F32),
        compiler_params=_cparams("parallel", "arbitrary"),
        name="peer_dense",
    )(ht, u, vt, cm, r1, m1)


def _ln2_kernel(h_ref, pt_ref, g_ref, b_ref, o_ref):
    o_ref[...] = _layer_norm(ALPHA * h_ref[...] + pt_ref[...].T, g_ref[...], b_ref[...])


def _ln2(h, peer_t, ln_g, ln_b):
    n = h.shape[0]
    tt = LN2_T
    return pl.pallas_call(
        _ln2_kernel,
        grid=(n // tt,),
        in_specs=[pl.BlockSpec((tt, D_MODEL), lambda t: (t, 0)),
                  pl.BlockSpec((D_MODEL, tt), lambda t: (0, t)),
                  _resident(ln_g.shape), _resident(ln_b.shape)],
        out_specs=pl.BlockSpec((tt, D_MODEL), lambda t: (t, 0)),
        out_shape=jax.ShapeDtypeStruct((n, D_MODEL), F32),
        compiler_params=_cparams("parallel"),
        name="ln2",
    )(h, peer_t, ln_g, ln_b)


def _prepare(w_in, sink, conv_w, w_mem_k, w_mem_v, w_attn_o, w_conv_out, w_mem_o, w_out,
             ln1_g, ln1_b, w_peer_q, peer_keys, peer_u, peer_v, ln2_g, ln2_b):
    cuts, off = {}, 0
    for name, width in (("q", ATTN_WIDTH), ("k", KV_WIDTH), ("v", KV_WIDTH), ("ch", CONV_WIDTH),
                        ("cb", CONV_WIDTH), ("cc", CONV_WIDTH), ("mq", MEM_WIDTH), ("gl", GATE_WIDTH)):
        cuts[name] = w_in[:, off:off + width]
        off += width
    w_a = jnp.concatenate([cuts[k] for k in ("q", "ch", "cb", "cc", "k", "v")], axis=1)
    w_b = jnp.concatenate([cuts[k] for k in ("gl", "mq")], axis=1)
    row = lambda v: v.reshape(1, -1).astype(F32)
    return dict(
        w_a=w_a.astype(MXU_DT), w_b=w_b.astype(MXU_DT), sink=sink.astype(F32), conv_w=conv_w.astype(F32),
        w_mem_k=w_mem_k.astype(MXU_DT), w_mem_v=w_mem_v.astype(MXU_DT),
        wa=w_attn_o.astype(MXU_DT), wc=w_conv_out.astype(MXU_DT), wm=w_mem_o.astype(MXU_DT),
        w_out=w_out.astype(MXU_DT), ln1_g=row(ln1_g), ln1_b=row(ln1_b),
        wq_t=w_peer_q.T.astype(MXU_DT),
        keys=peer_keys.reshape(PEER_HEADS * 2, N_KEYS, PEER_HALF).astype(MXU_DT),
        u=peer_u.astype(MXU_DT), vt=peer_v.T.astype(MXU_DT), ln2_g=row(ln2_g), ln2_b=row(ln2_b))


def _encode_group(x, mem, p):
    b, s, d = x.shape
    x2 = x.reshape(b * s, d)
    mem2 = mem.reshape(b * N_MEM, d)
    proj = _matmul(x2, p["w_a"], MXU_DT, PROJ_TM, PROJ_A_TN)
    proj_b = _matmul(x2, p["w_b"], MXU_DT, PROJ_TM, PROJ_B_TN)
    mem_k = _matmul(mem2, p["w_mem_k"], MXU_DT, 512, 512)
    mem_v = _matmul(mem2, p["w_mem_v"], MXU_DT, 512, 512)
    attn = _attention(proj, p["sink"], s)
    merged = _mixer(attn, proj, proj_b, mem_k, mem_v, p["conv_w"], p["wa"], p["wc"], p["wm"], s)
    h, ht, cm, r1, m1 = _ln1_route(x2, merged, p["w_out"], p["ln1_g"], p["ln1_b"], p["wq_t"], p["keys"])
    peer_t = _peer_dense(ht, p["u"], p["vt"], cm, r1, m1)
    return _ln2(h, peer_t, p["ln2_g"], p["ln2_b"]).reshape(b, s, d)


def kernel(x_prompt, x_sample, mem_prompt, mem_sample, w_in, sink, conv_w, w_mem_k, w_mem_v, w_attn_o,
           w_conv_out, w_mem_o, w_out, ln1_g, ln1_b, w_peer_q, peer_keys, peer_u, peer_v, ln2_g, ln2_b):
    assert w_in.shape[0] == DEPTH == 1
    p = _prepare(w_in[0], sink[0], conv_w[0], w_mem_k[0], w_mem_v[0], w_attn_o[0], w_conv_out[0],
                 w_mem_o[0], w_out[0], ln1_g[0], ln1_b[0], w_peer_q[0], peer_keys[0], peer_u[0],
                 peer_v[0], ln2_g[0], ln2_b[0])
    return (_encode_group(x_prompt, mem_prompt, p), _encode_group(x_sample, mem_sample, p))
```

```python
import functools
import math

import jax
import jax.numpy as jnp
from jax import lax
from jax.experimental import pallas as pl
from jax.experimental.pallas import tpu as pltpu

D_MODEL = 2048
N_Q_HEADS = 8
N_KV_HEADS = 2
GQA_GROUP = N_Q_HEADS // N_KV_HEADS
HEAD_DIM = 128
WINDOW = 128
ATTN_WIDTH = N_Q_HEADS * HEAD_DIM
KV_WIDTH = N_KV_HEADS * HEAD_DIM
CONV_WIDTH = 1024
N_MEM = 256
MEM_HEADS = 4
MEM_HEAD_DIM = 256
MEM_WIDTH = MEM_HEADS * MEM_HEAD_DIM
PEER_HEADS = 8
N_KEYS = 128
N_EXPERTS = N_KEYS * N_KEYS
PEER_HALF = 128
PEER_TOPK = 16
LN_EPS = 1e-5
DEPTH = 1
ALPHA = (2.0 * DEPTH) ** 0.25
GATE_WIDTH = 3 * D_MODEL
IN_WIDTH = ATTN_WIDTH + 2 * KV_WIDTH + 3 * CONV_WIDTH + MEM_WIDTH + GATE_WIDTH

COL_Q = 0
COL_CH = COL_Q + ATTN_WIDTH
COL_CB = COL_CH + CONV_WIDTH
COL_CC = COL_CB + CONV_WIDTH
COL_K = COL_CC + CONV_WIDTH
COL_V = COL_K + KV_WIDTH
PROJ_A_WIDTH = COL_V + KV_WIDTH
COL_GL = 0
COL_MQ = COL_GL + GATE_WIDTH
PROJ_B_WIDTH = COL_MQ + MEM_WIDTH

MXU_DT = jnp.bfloat16
GATE_DT = jnp.bfloat16
VMEM_LIMIT = 56 * 1024 * 1024

PROJ_TM = 1024
PROJ_A_TN = PROJ_A_WIDTH // 3
PROJ_B_TN = PROJ_B_WIDTH // 4
ATTN_BLOCKS = 8
MIX_T = 512
ROUTE_T = 256
ROUTE_LANES = 128
PEER_C = 512
PEER_EB = 256
PEER_SUB = 8
LN2_T = 512

F32 = jnp.float32


def _cparams(*sem):
    return pltpu.CompilerParams(dimension_semantics=sem, vmem_limit_bytes=VMEM_LIMIT)


def _resident(shape):
    nd = len(shape)
    return pl.BlockSpec(shape, lambda *_: (0,) * nd, pipeline_mode=pl.Buffered(1))


def _layer_norm(x, g, b):
    mu = jnp.mean(x, axis=-1, keepdims=True)
    xc = x - mu
    var = jnp.mean(xc * xc, axis=-1, keepdims=True)
    return xc * lax.rsqrt(var + LN_EPS) * g + b


def _mm_kernel(x_ref, w_ref, o_ref, xb_ref):
    @pl.when(pl.program_id(1) == 0)
    def _():
        xb_ref[...] = x_ref[...].astype(xb_ref.dtype)

    o_ref[...] = jnp.dot(xb_ref[...], w_ref[...], preferred_element_type=F32).astype(o_ref.dtype)


def _matmul(x, w, out_dtype, tm, tn):
    m, k = x.shape
    n = w.shape[1]
    tm, tn = min(tm, m), min(tn, n)
    assert m % tm == 0 and n % tn == 0
    return pl.pallas_call(
        _mm_kernel,
        grid=(m // tm, n // tn),
        in_specs=[pl.BlockSpec((tm, k), lambda i, j: (i, 0)),
                  pl.BlockSpec((k, tn), lambda i, j: (0, j))],
        out_specs=pl.BlockSpec((tm, tn), lambda i, j: (i, j)),
        out_shape=jax.ShapeDtypeStruct((m, n), out_dtype),
        scratch_shapes=[pltpu.VMEM((tm, k), w.dtype)],
        compiler_params=_cparams("parallel", "arbitrary"),
        name="matmul",
    )(x, w)


def _attn_kernel(sink_ref, q_ref, kp_ref, km_ref, kn_ref, vp_ref, vm_ref, vn_ref, o_ref, *, tiles_per_seq):
    t = pl.program_id(0)
    pos = t % tiles_per_seq
    lo = jnp.where(pos == 0, WINDOW, 0)
    hi = jnp.where(pos == tiles_per_seq - 1, 2 * WINDOW, 3 * WINDOW)
    kall = jnp.concatenate([kp_ref[...], km_ref[...], kn_ref[...]], axis=0)
    vall = jnp.concatenate([vp_ref[...], vm_ref[...], vn_ref[...]], axis=0)
    qi = lax.broadcasted_iota(jnp.int32, (WINDOW, 3 * WINDOW), 0)
    si = lax.broadcasted_iota(jnp.int32, (WINDOW, 3 * WINDOW), 1)
    dist = jnp.abs(qi + WINDOW - si)
    band = dist <= WINDOW
    distf = dist.astype(F32)
    scale = 1.0 / math.sqrt(HEAD_DIM)
    n_blocks = q_ref.shape[0] // WINDOW
    for r in range(n_blocks):
        valid = band
        if r == 0:
            valid = valid & (si >= lo)
        if r == n_blocks - 1:
            valid = valid & (si < hi)
        rows = slice(r * WINDOW, (r + 1) * WINDOW)
        for kv in range(N_KV_HEADS):
            cols = slice(kv * HEAD_DIM, (kv + 1) * HEAD_DIM)
            kc = kall[r * WINDOW:(r + 3) * WINDOW, cols]
            vc = vall[r * WINDOW:(r + 3) * WINDOW, cols]
            heads = [kv * GQA_GROUP + g for g in range(GQA_GROUP)]
            q4 = jnp.concatenate([q_ref[rows, h * HEAD_DIM:(h + 1) * HEAD_DIM] for h in heads], axis=0)
            s4 = lax.dot_general(q4, kc, (((1,), (1,)), ((), ())), preferred_element_type=F32) * scale
            ps, inv = [], []
            for g, h in enumerate(heads):
                slope = 2.0 ** (-8.0 * (h + 1) / N_Q_HEADS)
                s = jnp.where(valid, s4[g * WINDOW:(g + 1) * WINDOW] - slope * distf, -jnp.inf)
                sk = sink_ref[h]
                m = jnp.maximum(jnp.max(s, axis=-1, keepdims=True), sk)
                p = jnp.exp(s - m)
                denom = jnp.sum(p, axis=-1, keepdims=True) + jnp.exp(sk - m)
                ps.append(p.astype(vc.dtype))
                inv.append(1.0 / denom)
            o4 = jnp.dot(jnp.concatenate(ps, axis=0), vc, preferred_element_type=F32)
            for g, h in enumerate(heads):
                o_ref[rows, h * HEAD_DIM:(h + 1) * HEAD_DIM] = (
                    o4[g * WINDOW:(g + 1) * WINDOW] * inv[g]).astype(o_ref.dtype)


def _attention(proj, sink, seq_len):
    n = proj.shape[0]
    ta = ATTN_BLOCKS * WINDOW
    assert seq_len % ta == 0 and n % seq_len == 0
    nblk = n // WINDOW
    kcol, vcol = COL_K // KV_WIDTH, COL_V // KV_WIDTH

    def halo(col, side):
        if side < 0:
            return pl.BlockSpec((WINDOW, KV_WIDTH), lambda t: (jnp.maximum(t * ATTN_BLOCKS - 1, 0), col))
        return pl.BlockSpec((WINDOW, KV_WIDTH), lambda t: (jnp.minimum((t + 1) * ATTN_BLOCKS, nblk - 1), col))

    def main(col):
        return pl.BlockSpec((ta, KV_WIDTH), lambda t: (t, col))

    return pl.pallas_call(
        functools.partial(_attn_kernel, tiles_per_seq=seq_len // ta),
        grid=(n // ta,),
        in_specs=[pl.BlockSpec(memory_space=pltpu.SMEM),
                  pl.BlockSpec((ta, ATTN_WIDTH), lambda t: (t, COL_Q // ATTN_WIDTH)),
                  halo(kcol, -1), main(kcol), halo(kcol, 1),
                  halo(vcol, -1), main(vcol), halo(vcol, 1)],
        out_specs=pl.BlockSpec((ta, ATTN_WIDTH), lambda t: (t, 0)),
        out_shape=jax.ShapeDtypeStruct((n, ATTN_WIDTH), proj.dtype),
        compiler_params=_cparams("parallel"),
        name="attention",
    )(sink, proj, proj, proj, proj, proj, proj, proj)


def _mixer_kernel(attn_ref, ch_ref, cb_ref, cc_ref, chp_ref, ccp_ref, chn_ref, ccn_ref,
                  g0_ref, g1_ref, g2_ref, mq_ref, mk_ref, mv_ref, convw_ref,
                  wa_ref, wc_ref, wm_ref, o_ref, *, tiles_per_seq):
    t = pl.program_id(0)
    pos = t % tiles_per_seq
    tt = ch_ref.shape[0]
    z = cc_ref[...].astype(F32) * ch_ref[...].astype(F32)
    zp = ccp_ref[7:8, :].astype(F32) * chp_ref[7:8, :].astype(F32)
    zn = ccn_ref[0:1, :].astype(F32) * chn_ref[0:1, :].astype(F32)
    zp = jnp.where(pos == 0, 0.0, zp)
    zn = jnp.where(pos == tiles_per_seq - 1, 0.0, zn)
    row = lax.broadcasted_iota(jnp.int32, z.shape, 0)
    z_up = jnp.where(row == 0, zp, pltpu.roll(z, 1, 0))
    z_dn = jnp.where(row == tt - 1, zn, pltpu.roll(z, tt - 1, 0))
    conv = z_up * convw_ref[0:1, :] + z * convw_ref[1:2, :] + z_dn * convw_ref[2:3, :]
    cv = (cb_ref[...].astype(F32) * conv).astype(wc_ref.dtype)
    mscale = 1.0 / math.sqrt(MEM_HEAD_DIM)
    mo = []
    for hm in range(MEM_HEADS):
        cols = slice(hm * MEM_HEAD_DIM, (hm + 1) * MEM_HEAD_DIM)
        s = lax.dot_general(mq_ref[:, cols], mk_ref[:, cols], (((1,), (1,)), ((), ())),
                            preferred_element_type=F32) * mscale
        p = jnp.exp(s - jnp.max(s, axis=-1, keepdims=True))
        inv = 1.0 / jnp.sum(p, axis=-1, keepdims=True)
        o = jnp.dot(p.astype(mv_ref.dtype), mv_ref[:, cols], preferred_element_type=F32) * inv
        mo.append(o.astype(wm_ref.dtype))
    memo = jnp.concatenate(mo, axis=-1)
    a_out = jnp.dot(attn_ref[...], wa_ref[...], preferred_element_type=F32)
    c_out = jnp.dot(cv, wc_ref[...], preferred_element_type=F32)
    m_out = jnp.dot(memo, wm_ref[...], preferred_element_type=F32)
    merged = (jax.nn.sigmoid(g0_ref[...].astype(F32)) * a_out
              + jax.nn.sigmoid(g1_ref[...].astype(F32)) * c_out
              + jax.nn.sigmoid(g2_ref[...].astype(F32)) * m_out)
    o_ref[...] = merged.astype(o_ref.dtype)


def _mixer(attn, proj, proj_b, mem_k, mem_v, conv_w, wa, wc, wm, seq_len):
    n = attn.shape[0]
    tt = MIX_T
    assert seq_len % tt == 0
    tiles_per_seq = seq_len // tt
    n8 = n // 8

    def col(c, width):
        return pl.BlockSpec((tt, width), lambda t: (t, c // width))

    def prev8(c):
        return pl.BlockSpec((8, CONV_WIDTH), lambda t: (jnp.maximum(t * (tt // 8) - 1, 0), c // CONV_WIDTH))

    def next8(c):
        return pl.BlockSpec((8, CONV_WIDTH), lambda t: (jnp.minimum((t + 1) * (tt // 8), n8 - 1), c // CONV_WIDTH))

    def mem_spec():
        return pl.BlockSpec((N_MEM, MEM_WIDTH), lambda t: (t // tiles_per_seq, 0))

    return pl.pallas_call(
        functools.partial(_mixer_kernel, tiles_per_seq=tiles_per_seq),
        grid=(n // tt,),
        in_specs=[pl.BlockSpec((tt, ATTN_WIDTH), lambda t: (t, 0)),
                  col(COL_CH, CONV_WIDTH), col(COL_CB, CONV_WIDTH), col(COL_CC, CONV_WIDTH),
                  prev8(COL_CH), prev8(COL_CC), next8(COL_CH), next8(COL_CC),
                  col(COL_GL, D_MODEL), col(COL_GL + D_MODEL, D_MODEL), col(COL_GL + 2 * D_MODEL, D_MODEL),
                  col(COL_MQ, MEM_WIDTH), mem_spec(), mem_spec(),
                  _resident(conv_w.shape), _resident(wa.shape), _resident(wc.shape), _resident(wm.shape)],
        out_specs=pl.BlockSpec((tt, D_MODEL), lambda t: (t, 0)),
        out_shape=jax.ShapeDtypeStruct((n, D_MODEL), attn.dtype),
        compiler_params=_cparams("parallel"),
        name="mixer",
    )(attn, proj, proj, proj, proj, proj, proj, proj, proj_b, proj_b, proj_b, proj_b,
      mem_k, mem_v, conv_w, wa, wc, wm)


def _sort_network(n):
    pairs, p = [], 1
    while p < n:
        k = p
        while k >= 1:
            for j in range(k % p, n - k, 2 * k):
                for i in range(min(k, n - j - k)):
                    if (i + j) // (2 * p) == (i + j + k) // (2 * p):
                        pairs.append((i + j, i + j + k))
            k //= 2
        p *= 2
    return pairs


def _compare_exchange(v, i, j):
    v[i], v[j] = jnp.maximum(v[i], v[j]), jnp.minimum(v[i], v[j])


def _top16_rows(v, n_valid):
    assert 2 * n_valid >= PEER_TOPK
    v = list(v)
    for i, j in _sort_network(PEER_TOPK):
        if j < n_valid:
            _compare_exchange(v, i, j)
    for shift in (4, 2, 1):
        other = [pltpu.roll(x, shift, 0) for x in v[:n_valid]]
        merged = []
        for a in range(PEER_TOPK):
            b = PEER_TOPK - 1 - a
            if a < n_valid and b < n_valid:
                merged.append(jnp.maximum(v[a], other[b]))
            else:
                merged.append(v[a] if a < n_valid else other[b])
        v, n_valid = merged, PEER_TOPK
        for d in (8, 4, 2, 1):
            for a in range(PEER_TOPK):
                if a & d == 0:
                    _compare_exchange(v, a, a + d)
    return v


def _rows_to_block(rows, sub):
    out = rows[0]
    for k in range(1, 8):
        out = jnp.where(sub == k, rows[k], out)
    return out


def _all_sublanes(x, op):
    for shift in (4, 2, 1):
        x = op(x, pltpu.roll(x, shift, 0))
    return x


def _first_true(pred, thr):
    assert len(thr) == PEER_TOPK == 16
    q1 = pred(thr[7])
    q2 = pred(jnp.where(q1, thr[3], thr[11]))
    q3 = pred(jnp.where(q1, jnp.where(q2, thr[1], thr[5]), jnp.where(q2, thr[9], thr[13])))
    low = jnp.where(q2, jnp.where(q3, thr[0], thr[2]), jnp.where(q3, thr[4], thr[6]))
    high = jnp.where(q2, jnp.where(q3, thr[8], thr[10]), jnp.where(q3, thr[12], thr[14]))
    q4 = pred(jnp.where(q1, low, high))
    k = (jnp.where(q1, 0.0, 8.0) + jnp.where(q2, 0.0, 4.0)
         + jnp.where(q3, 0.0, 2.0) + jnp.where(q4, 0.0, 1.0))
    return jnp.where(pred(thr[15]), k, float(PEER_TOPK))


def _route_column(s0, s1):
    n_slab = N_KEYS // 8
    slabs0 = [s0[8 * v:8 * v + 8] for v in range(n_slab)]
    slabs1 = [s1[8 * v:8 * v + 8] for v in range(n_slab)]
    t0 = _top16_rows(slabs0, n_slab)
    t1 = _top16_rows(slabs1, n_slab)
    sub = lax.broadcasted_iota(jnp.int32, slabs0[0].shape, 0)
    v0 = (_rows_to_block(t0[:8], sub), _rows_to_block(t0[8:], sub))
    v1_lo = _rows_to_block(t1[:8], sub)
    cand = [t0[0] + v1_lo, t0[0] + _rows_to_block(t1[8:], sub), t0[1] + v1_lo]
    for a in range(2, 8):
        cand.append(jnp.where(sub < PEER_TOPK // (a + 1), t0[a] + v1_lo, -jnp.inf))
    cand.append(v0[1] + t1[0])
    thr = _top16_rows(cand + [None] * (PEER_TOPK - len(cand)), len(cand))[PEER_TOPK - 1]
    e0 = [jnp.exp(x - t0[0]) for x in v0]
    zacc = [jnp.zeros_like(thr), jnp.zeros_like(thr)]
    lowest = []
    for b in range(PEER_TOPK):
        e1b = jnp.exp(t1[b] - t1[0])
        low = None
        for half in range(2):
            sel = (v0[half] + t1[b]) >= thr
            zacc[half] = zacc[half] + jnp.where(sel, e1b, 0.0)
            cur = jnp.where(sel, v0[half], jnp.inf)
            low = cur if low is None else jnp.minimum(low, cur)
        lowest.append(_all_sublanes(low, jnp.minimum))
    inv_z = 1.0 / _all_sublanes(e0[0] * zacc[0] + e0[1] * zacc[1], jnp.add)
    c0, m0, r1, m1 = [], [], [], []
    for v in range(n_slab):
        c0.append(_first_true(lambda thr: slabs0[v] < thr, lowest))
        r1.append(_first_true(lambda thr: slabs1[v] >= thr, t1))
        m0.append(jnp.exp(slabs0[v] - t0[0]) * inv_z)
        m1.append(jnp.exp(slabs1[v] - t1[0]))
    cat = lambda xs: jnp.concatenate(xs, axis=0)
    return cat(c0), cat(m0), cat(r1), cat(m1)


COUNT_MASK = 0xFF


def _pack_count_factor(count, factor):
    assert jnp.dtype(GATE_DT).itemsize == 2
    top = lax.bitcast_convert_type(factor.astype(GATE_DT).astype(F32), jnp.uint32)
    return top | count.astype(jnp.int32).astype(jnp.uint32)


def _ln1_route_kernel(x_ref, mg_ref, wo_ref, g_ref, b_ref, wq_ref, keys_ref,
                      h_ref, ht_ref, cm_ref, r1_ref, m1_ref, qt_scr):
    mix = jnp.dot(mg_ref[...], wo_ref[...], preferred_element_type=F32)
    h = _layer_norm(ALPHA * x_ref[...] + mix, g_ref[...], b_ref[...])
    h_ref[...] = h
    ht = h.T.astype(ht_ref.dtype)
    ht_ref[...] = ht
    qt_scr[...] = jnp.dot(wq_ref[...], ht, preferred_element_type=F32).astype(qt_scr.dtype)

    def route(hd, lanes):
        base = pl.multiple_of(hd * (2 * PEER_HALF), 2 * PEER_HALF)
        s0 = jnp.dot(keys_ref[2 * hd], qt_scr[pl.ds(base, PEER_HALF), lanes], preferred_element_type=F32)
        s1 = jnp.dot(keys_ref[2 * hd + 1], qt_scr[pl.ds(base + PEER_HALF, PEER_HALF), lanes],
                     preferred_element_type=F32)
        c0, m0, r1, m1 = _route_column(s0, s1)
        out_rows = pl.ds(pl.multiple_of(hd * N_KEYS, N_KEYS), N_KEYS)
        cm_ref[out_rows, lanes] = _pack_count_factor(c0, m0)
        r1_ref[out_rows, lanes] = r1.astype(r1_ref.dtype)
        m1_ref[out_rows, lanes] = m1.astype(m1_ref.dtype)

    def head(hd, carry):
        for col in range(x_ref.shape[0] // ROUTE_LANES):
            route(hd, slice(col * ROUTE_LANES, (col + 1) * ROUTE_LANES))
        return carry

    lax.fori_loop(0, PEER_HEADS, head, 0)


def _ln1_route(x, merged, w_out, ln_g, ln_b, wq_t, keys):
    n = x.shape[0]
    tt = ROUTE_T
    rows = PEER_HEADS * N_KEYS
    tok = lambda t: (t, 0)
    tok_t = lambda t: (0, t)
    return pl.pallas_call(
        _ln1_route_kernel,
        grid=(n // tt,),
        in_specs=[pl.BlockSpec((tt, D_MODEL), tok), pl.BlockSpec((tt, D_MODEL), tok),
                  _resident(w_out.shape), _resident(ln_g.shape), _resident(ln_b.shape),
                  _resident(wq_t.shape), _resident(keys.shape)],
        out_specs=[pl.BlockSpec((tt, D_MODEL), tok), pl.BlockSpec((D_MODEL, tt), tok_t),
                   pl.BlockSpec((rows, tt), tok_t), pl.BlockSpec((rows, tt), tok_t),
                   pl.BlockSpec((rows, tt), tok_t)],
        out_shape=[jax.ShapeDtypeStruct((n, D_MODEL), F32),
                   jax.ShapeDtypeStruct((D_MODEL, n), MXU_DT),
                   jax.ShapeDtypeStruct((rows, n), jnp.uint32),
                   jax.ShapeDtypeStruct((rows, n), GATE_DT), jax.ShapeDtypeStruct((rows, n), GATE_DT)],
        scratch_shapes=[pltpu.VMEM((PEER_HEADS * 2 * PEER_HALF, tt), MXU_DT)],
        compiler_params=_cparams("parallel"),
        name="ln1_route",
    )(x, merged, w_out, ln_g, ln_b, wq_t, keys)


def _peer_gate_block(block, pre, cm_ref, r1_ref, m1_ref):
    act = (0.5 * pre * (1.0 + lax.erf(pre * math.sqrt(0.5)))).astype(GATE_DT)
    per_block = pre.shape[0] // N_KEYS
    lanes = pre.shape[1]
    tiles = (N_KEYS // 16, 16, lanes)
    out = []
    for ii in range(per_block):
        i = block * per_block + ii
        gate = None
        for hd in range(PEER_HEADS):
            word = jnp.broadcast_to(cm_ref[pl.ds(hd * N_KEYS + i, 1), :], (16, lanes))
            m0 = lax.bitcast_convert_type(word, F32).astype(GATE_DT)
            c0 = (word & COUNT_MASK).astype(jnp.int32).astype(F32).astype(GATE_DT)
            r1 = r1_ref[hd * N_KEYS:(hd + 1) * N_KEYS, :].reshape(tiles)
            m1 = m1_ref[hd * N_KEYS:(hd + 1) * N_KEYS, :].reshape(tiles)
            term = jnp.where(r1 < c0[None], m1, jnp.zeros_like(m1)) * m0[None]
            gate = term if gate is None else gate + term
        gate = gate.reshape(N_KEYS, lanes)
        out.append((gate * act[ii * N_KEYS:(ii + 1) * N_KEYS, :]).astype(MXU_DT))
    return jnp.concatenate(out, axis=0)


def _peer_kernel(ht_ref, u_ref, vt_ref, cm_ref, r1_ref, m1_ref, o_ref):
    g = pl.program_id(1)
    eb = PEER_EB
    n_sub = u_ref.shape[0] // eb

    @pl.when(g == 0)
    def _():
        o_ref[...] = jnp.zeros_like(o_ref)

    routing = (cm_ref, r1_ref, m1_ref)
    pre = [jnp.dot(u_ref[k * eb:(k + 1) * eb, :], ht_ref[...], preferred_element_type=F32)
           for k in range(n_sub)]
    for k in range(n_sub):
        w = _peer_gate_block(g * n_sub + k, pre[k], *routing)
        o_ref[...] += jnp.dot(vt_ref[:, k * eb:(k + 1) * eb], w, preferred_element_type=F32)


def _peer_dense(ht, u, vt, cm, r1, m1):
    n = ht.shape[1]
    c, step = PEER_C, PEER_EB * PEER_SUB
    rows = PEER_HEADS * N_KEYS
    tok_t = lambda t, g: (0, t)
    tile = lambda r: pl.BlockSpec((r, c), tok_t)
    return pl.pallas_call(
        _peer_kernel,
        grid=(n // c, N_EXPERTS // step),
        in_specs=[tile(D_MODEL),
                  pl.BlockSpec((step, D_MODEL), lambda t, g: (g, 0)),
                  pl.BlockSpec((D_MODEL, step), lambda t, g: (0, g)),
                  tile(rows), tile(rows), tile(rows)],
        out_specs=tile(D_MODEL),
        out_shape=jax.ShapeDtypeStruct((D_MODEL, n), F32),
        compiler_params=_cparams("parallel", "arbitrary"),
        name="peer_dense",
    )(ht, u, vt, cm, r1, m1)


def _ln2_kernel(h_ref, pt_ref, g_ref, b_ref, o_ref):
    o_ref[...] = _layer_norm(ALPHA * h_ref[...] + pt_ref[...].T, g_ref[...], b_ref[...])


def _ln2(h, peer_t, ln_g, ln_b):
    n = h.shape[0]
    tt = LN2_T
    return pl.pallas_call(
        _ln2_kernel,
        grid=(n // tt,),
        in_specs=[pl.BlockSpec((tt, D_MODEL), lambda t: (t, 0)),
                  pl.BlockSpec((D_MODEL, tt), lambda t: (0, t)),
                  _resident(ln_g.shape), _resident(ln_b.shape)],
        out_specs=pl.BlockSpec((tt, D_MODEL), lambda t: (t, 0)),
        out_shape=jax.ShapeDtypeStruct((n, D_MODEL), F32),
        compiler_params=_cparams("parallel"),
        name="ln2",
    )(h, peer_t, ln_g, ln_b)


def _prepare(w_in, sink, conv_w, w_mem_k, w_mem_v, w_attn_o, w_conv_out, w_mem_o, w_out,
             ln1_g, ln1_b, w_peer_q, peer_keys, peer_u, peer_v, ln2_g, ln2_b):
    cuts, off = {}, 0
    for name, width in (("q", ATTN_WIDTH), ("k", KV_WIDTH), ("v", KV_WIDTH), ("ch", CONV_WIDTH),
                        ("cb", CONV_WIDTH), ("cc", CONV_WIDTH), ("mq", MEM_WIDTH), ("gl", GATE_WIDTH)):
        cuts[name] = w_in[:, off:off + width]
        off += width
    w_a = jnp.concatenate([cuts[k] for k in ("q", "ch", "cb", "cc", "k", "v")], axis=1)
    w_b = jnp.concatenate([cuts[k] for k in ("gl", "mq")], axis=1)
    row = lambda v: v.reshape(1, -1).astype(F32)
    return dict(
        w_a=w_a.astype(MXU_DT), w_b=w_b.astype(MXU_DT), sink=sink.astype(F32), conv_w=conv_w.astype(F32),
        w_mem_k=w_mem_k.astype(MXU_DT), w_mem_v=w_mem_v.astype(MXU_DT),
        wa=w_attn_o.astype(MXU_DT), wc=w_conv_out.astype(MXU_DT), wm=w_mem_o.astype(MXU_DT),
        w_out=w_out.astype(MXU_DT), ln1_g=row(ln1_g), ln1_b=row(ln1_b),
        wq_t=w_peer_q.T.astype(MXU_DT),
        keys=peer_keys.reshape(PEER_HEADS * 2, N_KEYS, PEER_HALF).astype(MXU_DT),
        u=peer_u.astype(MXU_DT), vt=peer_v.T.astype(MXU_DT), ln2_g=row(ln2_g), ln2_b=row(ln2_b))


def _encode_group(x, mem, p):
    b, s, d = x.shape
    x2 = x.reshape(b * s, d)
    mem2 = mem.reshape(b * N_MEM, d)
    proj = _matmul(x2, p["w_a"], MXU_DT, PROJ_TM, PROJ_A_TN)
    proj_b = _matmul(x2, p["w_b"], MXU_DT, PROJ_TM, PROJ_B_TN)
    mem_k = _matmul(mem2, p["w_mem_k"], MXU_DT, 512, 512)
    mem_v = _matmul(mem2, p["w_mem_v"], MXU_DT, 512, 512)
    attn = _attention(proj, p["sink"], s)
    merged = _mixer(attn, proj, proj_b, mem_k, mem_v, p["conv_w"], p["wa"], p["wc"], p["wm"], s)
    h, ht, cm, r1, m1 = _ln1_route(x2, merged, p["w_out"], p["ln1_g"], p["ln1_b"], p["wq_t"], p["keys"])
    peer_t = _peer_dense(ht, p["u"], p["vt"], cm, r1, m1)
    return _ln2(h, peer_t, p["ln2_g"], p["ln2_b"]).reshape(b, s, d)


def kernel(x_prompt, x_sample, mem_prompt, mem_sample, w_in, sink, conv_w, w_mem_k, w_mem_v, w_attn_o,
           w_conv_out, w_mem_o, w_out, ln1_g, ln1_b, w_peer_q, peer_keys, peer_u, peer_v, ln2_g, ln2_b):
    assert w_in.shape[0] == DEPTH == 1
    p = _prepare(w_in[0], sink[0], conv_w[0], w_mem_k[0], w_mem_v[0], w_attn_o[0], w_conv_out[0],
                 w_mem_o[0], w_out[0], ln1_g[0], ln1_b[0], w_peer_q[0], peer_keys[0], peer_u[0],
                 peer_v[0], ln2_g[0], ln2_b[0])
    return (_encode_group(x_prompt, mem_prompt, p), _encode_group(x_sample, mem_sample, p))
```

```python
import functools
import math

import jax
import jax.numpy as jnp
from jax import lax
from jax.experimental import pallas as pl
from jax.experimental.pallas import tpu as pltpu

D_MODEL = 2048
N_Q_HEADS = 8
N_KV_HEADS = 2
GQA_GROUP = N_Q_HEADS // N_KV_HEADS
HEAD_DIM = 128
WINDOW = 128
ATTN_WIDTH = N_Q_HEADS * HEAD_DIM
KV_WIDTH = N_KV_HEADS * HEAD_DIM
CONV_WIDTH = 1024
N_MEM = 256
MEM_HEADS = 4
MEM_HEAD_DIM = 256
MEM_WIDTH = MEM_HEADS * MEM_HEAD_DIM
PEER_HEADS = 8
N_KEYS = 128
N_EXPERTS = N_KEYS * N_KEYS
PEER_HALF = 128
PEER_TOPK = 16
LN_EPS = 1e-5
DEPTH = 1
ALPHA = (2.0 * DEPTH) ** 0.25
GATE_WIDTH = 3 * D_MODEL

COL_Q = 0
COL_CH = COL_Q + ATTN_WIDTH
COL_CB = COL_CH + CONV_WIDTH
COL_CC = COL_CB + CONV_WIDTH
COL_K = COL_CC + CONV_WIDTH
COL_V = COL_K + KV_WIDTH
PROJ_A_WIDTH = COL_V + KV_WIDTH
COL_GL = 0
COL_MQ = COL_GL + GATE_WIDTH
PROJ_B_WIDTH = COL_MQ + MEM_WIDTH

MXU_DT = jnp.bfloat16
GATE_DT = jnp.bfloat16
VMEM_LIMIT = 56 * 1024 * 1024

PROJ_TM = 1024
PROJ_A_TN = PROJ_A_WIDTH // 3
PROJ_B_TN = PROJ_B_WIDTH // 4
ATTN_BLOCKS = 8
MIX_T = 512
ROUTE_T = 256
ROUTE_LANES = 128
PEER_C = 512
PEER_EB = 256
PEER_SUB = 8
LN2_T = 512

F32 = jnp.float32


def _cparams(*sem):
    return pltpu.CompilerParams(dimension_semantics=sem, vmem_limit_bytes=VMEM_LIMIT)


def _resident(shape):
    nd = len(shape)
    return pl.BlockSpec(shape, lambda *_: (0,) * nd, pipeline_mode=pl.Buffered(1))


def _layer_norm(x, g, b):
    mu = jnp.mean(x, axis=-1, keepdims=True)
    xc = x - mu
    var = jnp.mean(xc * xc, axis=-1, keepdims=True)
    return xc * lax.rsqrt(var + LN_EPS) * g + b


def _mm_kernel(x_ref, w_ref, o_ref, xb_ref):
    @pl.when(pl.program_id(1) == 0)
    def _():
        xb_ref[...] = x_ref[...].astype(xb_ref.dtype)

    o_ref[...] = jnp.dot(xb_ref[...], w_ref[...], preferred_element_type=F32).astype(o_ref.dtype)


def _matmul(x, w, out_dtype, tm, tn):
    m, k = x.shape
    n = w.shape[1]
    tm, tn = min(tm, m), min(tn, n)
    assert m % tm == 0 and n % tn == 0
    return pl.pallas_call(
        _mm_kernel,
        grid=(m // tm, n // tn),
        in_specs=[pl.BlockSpec((tm, k), lambda i, j: (i, 0)),
                  pl.BlockSpec((k, tn), lambda i, j: (0, j))],
        out_specs=pl.BlockSpec((tm, tn), lambda i, j: (i, j)),
        out_shape=jax.ShapeDtypeStruct((m, n), out_dtype),
        scratch_shapes=[pltpu.VMEM((tm, k), w.dtype)],
        compiler_params=_cparams("parallel", "arbitrary"),
        name="matmul",
    )(x, w)


def _attn_kernel(sink_ref, q_ref, kp_ref, km_ref, kn_ref, vp_ref, vm_ref, vn_ref, o_ref, *, tiles_per_seq):
    t = pl.program_id(0)
    pos = t % tiles_per_seq
    lo = jnp.where(pos == 0, WINDOW, 0)
    hi = jnp.where(pos == tiles_per_seq - 1, 2 * WINDOW, 3 * WINDOW)
    kall = jnp.concatenate([kp_ref[...], km_ref[...], kn_ref[...]], axis=0)
    vall = jnp.concatenate([vp_ref[...], vm_ref[...], vn_ref[...]], axis=0)
    qi = lax.broadcasted_iota(jnp.int32, (WINDOW, 3 * WINDOW), 0)
    si = lax.broadcasted_iota(jnp.int32, (WINDOW, 3 * WINDOW), 1)
    dist = jnp.abs(qi + WINDOW - si)
    band = dist <= WINDOW
    distf = dist.astype(F32)
    scale = 1.0 / math.sqrt(HEAD_DIM)
    n_blocks = q_ref.shape[0] // WINDOW
    for r in range(n_blocks):
        valid = band
        if r == 0:
            valid = valid & (si >= lo)
        if r == n_blocks - 1:
            valid = valid & (si < hi)
        rows = slice(r * WINDOW, (r + 1) * WINDOW)
        for kv in range(N_KV_HEADS):
            cols = slice(kv * HEAD_DIM, (kv + 1) * HEAD_DIM)
            kc = kall[r * WINDOW:(r + 3) * WINDOW, cols]
            vc = vall[r * WINDOW:(r + 3) * WINDOW, cols]
            heads = [kv * GQA_GROUP + g for g in range(GQA_GROUP)]
            q4 = jnp.concatenate([q_ref[rows, h * HEAD_DIM:(h + 1) * HEAD_DIM] for h in heads], axis=0)
            s4 = lax.dot_general(q4, kc, (((1,), (1,)), ((), ())), preferred_element_type=F32) * scale
            ps, inv = [], []
            for g, h in enumerate(heads):
                slope = 2.0 ** (-8.0 * (h + 1) / N_Q_HEADS)
                s = jnp.where(valid, s4[g * WINDOW:(g + 1) * WINDOW] - slope * distf, -jnp.inf)
                sk = sink_ref[h]
                m = jnp.maximum(jnp.max(s, axis=-1, keepdims=True), sk)
                p = jnp.exp(s - m)
                denom = jnp.sum(p, axis=-1, keepdims=True) + jnp.exp(sk - m)
                ps.append(p.astype(vc.dtype))
                inv.append(1.0 / denom)
            o4 = jnp.dot(jnp.concatenate(ps, axis=0), vc, preferred_element_type=F32)
            for g, h in enumerate(heads):
                o_ref[rows, h * HEAD_DIM:(h + 1) * HEAD_DIM] = (
                    o4[g * WINDOW:(g + 1) * WINDOW] * inv[g]).astype(o_ref.dtype)


def _attention(proj, sink, seq_len):
    n = proj.shape[0]
    ta = ATTN_BLOCKS * WINDOW
    assert seq_len % ta == 0 and n % seq_len == 0
    nblk = n // WINDOW
    kcol, vcol = COL_K // KV_WIDTH, COL_V // KV_WIDTH

    def halo(col, side):
        if side < 0:
            return pl.BlockSpec((WINDOW, KV_WIDTH), lambda t: (jnp.maximum(t * ATTN_BLOCKS - 1, 0), col))
        return pl.BlockSpec((WINDOW, KV_WIDTH), lambda t: (jnp.minimum((t + 1) * ATTN_BLOCKS, nblk - 1), col))

    def main(col):
        return pl.BlockSpec((ta, KV_WIDTH), lambda t: (t, col))

    return pl.pallas_call(
        functools.partial(_attn_kernel, tiles_per_seq=seq_len // ta),
        grid=(n // ta,),
        in_specs=[pl.BlockSpec(memory_space=pltpu.SMEM),
                  pl.BlockSpec((ta, ATTN_WIDTH), lambda t: (t, COL_Q // ATTN_WIDTH)),
                  halo(kcol, -1), main(kcol), halo(kcol, 1),
                  halo(vcol, -1), main(vcol), halo(vcol, 1)],
        out_specs=pl.BlockSpec((ta, ATTN_WIDTH), lambda t: (t, 0)),
        out_shape=jax.ShapeDtypeStruct((n, ATTN_WIDTH), proj.dtype),
        compiler_params=_cparams("parallel"),
        name="attention",
    )(sink, proj, proj, proj, proj, proj, proj, proj)


def _mixer_kernel(attn_ref, ch_ref, cb_ref, cc_ref, chp_ref, ccp_ref, chn_ref, ccn_ref,
                  g0_ref, g1_ref, g2_ref, mq_ref, mk_ref, mv_ref, convw_ref,
                  wa_ref, wc_ref, wm_ref, o_ref, *, tiles_per_seq):
    t = pl.program_id(0)
    pos = t % tiles_per_seq
    tt = ch_ref.shape[0]
    z = cc_ref[...].astype(F32) * ch_ref[...].astype(F32)
    zp = ccp_ref[7:8, :].astype(F32) * chp_ref[7:8, :].astype(F32)
    zn = ccn_ref[0:1, :].astype(F32) * chn_ref[0:1, :].astype(F32)
    zp = jnp.where(pos == 0, 0.0, zp)
    zn = jnp.where(pos == tiles_per_seq - 1, 0.0, zn)
    row = lax.broadcasted_iota(jnp.int32, z.shape, 0)
    z_up = jnp.where(row == 0, zp, pltpu.roll(z, 1, 0))
    z_dn = jnp.where(row == tt - 1, zn, pltpu.roll(z, tt - 1, 0))
    conv = z_up * convw_ref[0:1, :] + z * convw_ref[1:2, :] + z_dn * convw_ref[2:3, :]
    cv = (cb_ref[...].astype(F32) * conv).astype(wc_ref.dtype)
    mscale = 1.0 / math.sqrt(MEM_HEAD_DIM)
    mo = []
    for hm in range(MEM_HEADS):
        cols = slice(hm * MEM_HEAD_DIM, (hm + 1) * MEM_HEAD_DIM)
        s = lax.dot_general(mq_ref[:, cols], mk_ref[:, cols], (((1,), (1,)), ((), ())),
                            preferred_element_type=F32) * mscale
        p = jnp.exp(s - jnp.max(s, axis=-1, keepdims=True))
        inv = 1.0 / jnp.sum(p, axis=-1, keepdims=True)
        o = jnp.dot(p.astype(mv_ref.dtype), mv_ref[:, cols], preferred_element_type=F32) * inv
        mo.append(o.astype(wm_ref.dtype))
    memo = jnp.concatenate(mo, axis=-1)
    a_out = jnp.dot(attn_ref[...], wa_ref[...], preferred_element_type=F32)
    c_out = jnp.dot(cv, wc_ref[...], preferred_element_type=F32)
    m_out = jnp.dot(memo, wm_ref[...], preferred_element_type=F32)
    merged = (jax.nn.sigmoid(g0_ref[...].astype(F32)) * a_out
              + jax.nn.sigmoid(g1_ref[...].astype(F32)) * c_out
              + jax.nn.sigmoid(g2_ref[...].astype(F32)) * m_out)
    o_ref[...] = merged.astype(o_ref.dtype)


def _mixer(attn, proj, proj_b, mem_k, mem_v, conv_w, wa, wc, wm, seq_len):
    n = attn.shape[0]
    tt = MIX_T
    assert seq_len % tt == 0
    tiles_per_seq = seq_len // tt
    n8 = n // 8

    def col(c, width):
        return pl.BlockSpec((tt, width), lambda t: (t, c // width))

    def prev8(c):
        return pl.BlockSpec((8, CONV_WIDTH), lambda t: (jnp.maximum(t * (tt // 8) - 1, 0), c // CONV_WIDTH))

    def next8(c):
        return pl.BlockSpec((8, CONV_WIDTH), lambda t: (jnp.minimum((t + 1) * (tt // 8), n8 - 1), c // CONV_WIDTH))

    def mem_spec():
        return pl.BlockSpec((N_MEM, MEM_WIDTH), lambda t: (t // tiles_per_seq, 0))

    return pl.pallas_call(
        functools.partial(_mixer_kernel, tiles_per_seq=tiles_per_seq),
        grid=(n // tt,),
        in_specs=[pl.BlockSpec((tt, ATTN_WIDTH), lambda t: (t, 0)),
                  col(COL_CH, CONV_WIDTH), col(COL_CB, CONV_WIDTH), col(COL_CC, CONV_WIDTH),
                  prev8(COL_CH), prev8(COL_CC), next8(COL_CH), next8(COL_CC),
                  col(COL_GL, D_MODEL), col(COL_GL + D_MODEL, D_MODEL), col(COL_GL + 2 * D_MODEL, D_MODEL),
                  col(COL_MQ, MEM_WIDTH), mem_spec(), mem_spec(),
                  _resident(conv_w.shape), _resident(wa.shape), _resident(wc.shape), _resident(wm.shape)],
        out_specs=pl.BlockSpec((tt, D_MODEL), lambda t: (t, 0)),
        out_shape=jax.ShapeDtypeStruct((n, D_MODEL), attn.dtype),
        compiler_params=_cparams("parallel"),
        name="mixer",
    )(attn, proj, proj, proj, proj, proj, proj, proj, proj_b, proj_b, proj_b, proj_b,
      mem_k, mem_v, conv_w, wa, wc, wm)


def _sort_network(n):
    pairs, p = [], 1
    while p < n:
        k = p
        while k >= 1:
            for j in range(k % p, n - k, 2 * k):
                for i in range(min(k, n - j - k)):
                    if (i + j) // (2 * p) == (i + j + k) // (2 * p):
                        pairs.append((i + j, i + j + k))
            k //= 2
        p *= 2
    return pairs


def _compare_exchange(v, i, j):
    v[i], v[j] = jnp.maximum(v[i], v[j]), jnp.minimum(v[i], v[j])


def _top16_rows(v, n_valid):
    assert 2 * n_valid >= PEER_TOPK
    v = list(v)
    for i, j in _sort_network(PEER_TOPK):
        if j < n_valid:
            _compare_exchange(v, i, j)
    for shift in (4, 2, 1):
        other = [pltpu.roll(x, shift, 0) for x in v[:n_valid]]
        merged = []
        for a in range(PEER_TOPK):
            b = PEER_TOPK - 1 - a
            if a < n_valid and b < n_valid:
                merged.append(jnp.maximum(v[a], other[b]))
            else:
                merged.append(v[a] if a < n_valid else other[b])
        v, n_valid = merged, PEER_TOPK
        for d in (8, 4, 2, 1):
            for a in range(PEER_TOPK):
                if a & d == 0:
                    _compare_exchange(v, a, a + d)
    return v


def _rows_to_block(rows, sub):
    out = rows[0]
    for k in range(1, 8):
        out = jnp.where(sub == k, rows[k], out)
    return out


def _all_sublanes(x, op):
    for shift in (4, 2, 1):
        x = op(x, pltpu.roll(x, shift, 0))
    return x


def _first_true(pred, thr):
    assert len(thr) == PEER_TOPK == 16
    q1 = pred(thr[7])
    q2 = pred(jnp.where(q1, thr[3], thr[11]))
    q3 = pred(jnp.where(q1, jnp.where(q2, thr[1], thr[5]), jnp.where(q2, thr[9], thr[13])))
    low = jnp.where(q2, jnp.where(q3, thr[0], thr[2]), jnp.where(q3, thr[4], thr[6]))
    high = jnp.where(q2, jnp.where(q3, thr[8], thr[10]), jnp.where(q3, thr[12], thr[14]))
    q4 = pred(jnp.where(q1, low, high))
    k = (jnp.where(q1, 0.0, 8.0) + jnp.where(q2, 0.0, 4.0)
         + jnp.where(q3, 0.0, 2.0) + jnp.where(q4, 0.0, 1.0))
    return jnp.where(pred(thr[15]), k, float(PEER_TOPK))


def _route_column(s0, s1):
    n_slab = N_KEYS // 8
    slabs0 = [s0[8 * v:8 * v + 8] for v in range(n_slab)]
    slabs1 = [s1[8 * v:8 * v + 8] for v in range(n_slab)]
    t0 = _top16_rows(slabs0, n_slab)
    t1 = _top16_rows(slabs1, n_slab)
    sub = lax.broadcasted_iota(jnp.int32, slabs0[0].shape, 0)
    v0 = (_rows_to_block(t0[:8], sub), _rows_to_block(t0[8:], sub))
    v1_lo = _rows_to_block(t1[:8], sub)
    cand = [t0[0] + v1_lo, t0[0] + _rows_to_block(t1[8:], sub), t0[1] + v1_lo]
    for a in range(2, 8):
        cand.append(jnp.where(sub < PEER_TOPK // (a + 1), t0[a] + v1_lo, -jnp.inf))
    cand.append(v0[1] + t1[0])
    thr = _top16_rows(cand + [None] * (PEER_TOPK - len(cand)), len(cand))[PEER_TOPK - 1]
    e0 = [jnp.exp(x - t0[0]) for x in v0]
    zacc = [jnp.zeros_like(thr), jnp.zeros_like(thr)]
    lowest = []
    for b in range(PEER_TOPK):
        e1b = jnp.exp(t1[b] - t1[0])
        low = None
        for half in range(2):
            sel = (v0[half] + t1[b]) >= thr
            zacc[half] = zacc[half] + jnp.where(sel, e1b, 0.0)
            cur = jnp.where(sel, v0[half], jnp.inf)
            low = cur if low is None else jnp.minimum(low, cur)
        lowest.append(_all_sublanes(low, jnp.minimum))
    inv_z = 1.0 / _all_sublanes(e0[0] * zacc[0] + e0[1] * zacc[1], jnp.add)
    c0, m0, r1, m1 = [], [], [], []
    for v in range(n_slab):
        c0.append(_first_true(lambda thr: slabs0[v] < thr, lowest))
        r1.append(_first_true(lambda thr: slabs1[v] >= thr, t1))
        m0.append(jnp.exp(slabs0[v] - t0[0]) * inv_z)
        m1.append(jnp.exp(slabs1[v] - t1[0]))
    cat = lambda xs: jnp.concatenate(xs, axis=0)
    return cat(c0), cat(m0), cat(r1), cat(m1)


COUNT_MASK = 0xFF


def _pack_count_factor(count, factor):
    assert jnp.dtype(GATE_DT).itemsize == 2
    top = lax.bitcast_convert_type(factor.astype(GATE_DT).astype(F32), jnp.uint32)
    return top | count.astype(jnp.int32).astype(jnp.uint32)


def _ln1_route_kernel(x_ref, mg_ref, wo_ref, g_ref, b_ref, wq_ref, keys_ref,
                      h_ref, ht_ref, cm_ref, r1_ref, m1_ref, qt_scr):
    mix = jnp.dot(mg_ref[...], wo_ref[...], preferred_element_type=F32)
    h = _layer_norm(ALPHA * x_ref[...] + mix, g_ref[...], b_ref[...])
    h_ref[...] = h
    ht = h.T.astype(ht_ref.dtype)
    ht_ref[...] = ht
    qt_scr[...] = jnp.dot(wq_ref[...], ht, preferred_element_type=F32).astype(qt_scr.dtype)

    def route(hd, lanes):
        base = pl.multiple_of(hd * (2 * PEER_HALF), 2 * PEER_HALF)
        s0 = jnp.dot(keys_ref[2 * hd], qt_scr[pl.ds(base, PEER_HALF), lanes], preferred_element_type=F32)
        s1 = jnp.dot(keys_ref[2 * hd + 1], qt_scr[pl.ds(base + PEER_HALF, PEER_HALF), lanes],
                     preferred_element_type=F32)
        c0, m0, r1, m1 = _route_column(s0, s1)
        out_rows = pl.ds(pl.multiple_of(hd * N_KEYS, N_KEYS), N_KEYS)
        cm_ref[out_rows, lanes] = _pack_count_factor(c0, m0)
        r1_ref[out_rows, lanes] = r1.astype(r1_ref.dtype)
        m1_ref[out_rows, lanes] = m1.astype(m1_ref.dtype)

    def head(hd, carry):
        for col in range(x_ref.shape[0] // ROUTE_LANES):
            route(hd, slice(col * ROUTE_LANES, (col + 1) * ROUTE_LANES))
        return carry

    lax.fori_loop(0, PEER_HEADS, head, 0)


def _ln1_route(x, merged, w_out, ln_g, ln_b, wq_t, keys):
    n = x.shape[0]
    tt = ROUTE_T
    rows = PEER_HEADS * N_KEYS
    tok = lambda t: (t, 0)
    tok_t = lambda t: (0, t)
    return pl.pallas_call(
        _ln1_route_kernel,
        grid=(n // tt,),
        in_specs=[pl.BlockSpec((tt, D_MODEL), tok), pl.BlockSpec((tt, D_MODEL), tok),
                  _resident(w_out.shape), _resident(ln_g.shape), _resident(ln_b.shape),
                  _resident(wq_t.shape), _resident(keys.shape)],
        out_specs=[pl.BlockSpec((tt, D_MODEL), tok), pl.BlockSpec((D_MODEL, tt), tok_t),
                   pl.BlockSpec((rows, tt), tok_t), pl.BlockSpec((rows, tt), tok_t),
                   pl.BlockSpec((rows, tt), tok_t)],
        out_shape=[jax.ShapeDtypeStruct((n, D_MODEL), F32),
                   jax.ShapeDtypeStruct((D_MODEL, n), MXU_DT),
                   jax.ShapeDtypeStruct((rows, n), jnp.uint32),
                   jax.ShapeDtypeStruct((rows, n), GATE_DT), jax.ShapeDtypeStruct((rows, n), GATE_DT)],
        scratch_shapes=[pltpu.VMEM((PEER_HEADS * 2 * PEER_HALF, tt), MXU_DT)],
        compiler_params=_cparams("parallel"),
        name="ln1_route",
    )(x, merged, w_out, ln_g, ln_b, wq_t, keys)


def _peer_gate_block(block, pre, cm_ref, r1_ref, m1_ref):
    act = (0.5 * pre * (1.0 + lax.erf(pre * math.sqrt(0.5)))).astype(GATE_DT)
    per_block = pre.shape[0] // N_KEYS
    lanes = pre.shape[1]
    tiles = (N_KEYS // 16, 16, lanes)
    out = []
    for ii in range(per_block):
        i = block * per_block + ii
        gate = None
        for hd in range(PEER_HEADS):
            word = jnp.broadcast_to(cm_ref[pl.ds(hd * N_KEYS + i, 1), :], (16, lanes))
            m0 = lax.bitcast_convert_type(word, F32).astype(GATE_DT)
            c0 = (word & COUNT_MASK).astype(jnp.int32).astype(F32).astype(GATE_DT)
            r1 = r1_ref[hd * N_KEYS:(hd + 1) * N_KEYS, :].reshape(tiles)
            m1 = m1_ref[hd * N_KEYS:(hd + 1) * N_KEYS, :].reshape(tiles)
            term = jnp.where(r1 < c0[None], m1, jnp.zeros_like(m1)) * m0[None]
            gate = term if gate is None else gate + term
        gate = gate.reshape(N_KEYS, lanes)
        out.append((gate * act[ii * N_KEYS:(ii + 1) * N_KEYS, :]).astype(MXU_DT))
    return jnp.concatenate(out, axis=0)


def _peer_kernel(ht_ref, u_ref, vt_ref, cm_ref, r1_ref, m1_ref, o_ref):
    g = pl.program_id(1)
    eb = PEER_EB
    n_sub = u_ref.shape[0] // eb

    @pl.when(g == 0)
    def _():
        o_ref[...] = jnp.zeros_like(o_ref)

    routing = (cm_ref, r1_ref, m1_ref)
    pre = [jnp.dot(u_ref[k * eb:(k + 1) * eb, :], ht_ref[...], preferred_element_type=F32)
           for k in range(n_sub)]
    for k in range(n_sub):
        w = _peer_gate_block(g * n_sub + k, pre[k], *routing)
        o_ref[...] += jnp.dot(vt_ref[:, k * eb:(k + 1) * eb], w, preferred_element_type=F32)


def _peer_dense(ht, u, vt, cm, r1, m1):
    n = ht.shape[1]
    c, step = PEER_C, PEER_EB * PEER_SUB
    rows = PEER_HEADS * N_KEYS
    tok_t = lambda t, g: (0, t)
    tile = lambda r: pl.BlockSpec((r, c), tok_t)
    return pl.pallas_call(
        _peer_kernel,
        grid=(n // c, N_EXPERTS // step),
        in_specs=[tile(D_MODEL),
                  pl.BlockSpec((step, D_MODEL), lambda t, g: (g, 0)),
                  pl.BlockSpec((None, D_MODEL, step), lambda t, g: (g, 0, 0)),
                  tile(rows), tile(rows), tile(rows)],
        out_specs=tile(D_MODEL),
        out_shape=jax.ShapeDtypeStruct((D_MODEL, n), F32),
        compiler_params=_cparams("parallel", "arbitrary"),
        name="peer_dense",
    )(ht, u, vt, cm, r1, m1)


def _ln2_kernel(h_ref, pt_ref, g_ref, b_ref, o_ref):
    o_ref[...] = _layer_norm(ALPHA * h_ref[...] + pt_ref[...].T, g_ref[...], b_ref[...])


def _ln2(h, peer_t, ln_g, ln_b):
    n = h.shape[0]
    tt = LN2_T
    return pl.pallas_call(
        _ln2_kernel,
        grid=(n // tt,),
        in_specs=[pl.BlockSpec((tt, D_MODEL), lambda t: (t, 0)),
                  pl.BlockSpec((D_MODEL, tt), lambda t: (0, t)),
                  _resident(ln_g.shape), _resident(ln_b.shape)],
        out_specs=pl.BlockSpec((tt, D_MODEL), lambda t: (t, 0)),
        out_shape=jax.ShapeDtypeStruct((n, D_MODEL), F32),
        compiler_params=_cparams("parallel"),
        name="ln2",
    )(h, peer_t, ln_g, ln_b)


def _step_major_transpose(table):
    step = PEER_EB * PEER_SUB
    return table.reshape(N_EXPERTS // step, step, table.shape[1]).transpose(0, 2, 1)


def _prepare(w_in, sink, conv_w, w_mem_k, w_mem_v, w_attn_o, w_conv_out, w_mem_o, w_out,
             ln1_g, ln1_b, w_peer_q, peer_keys, peer_u, peer_v, ln2_g, ln2_b):
    cuts, off = {}, 0
    for name, width in (("q", ATTN_WIDTH), ("k", KV_WIDTH), ("v", KV_WIDTH), ("ch", CONV_WIDTH),
                        ("cb", CONV_WIDTH), ("cc", CONV_WIDTH), ("mq", MEM_WIDTH), ("gl", GATE_WIDTH)):
        cuts[name] = w_in[:, off:off + width]
        off += width
    w_a = jnp.concatenate([cuts[k] for k in ("q", "ch", "cb", "cc", "k", "v")], axis=1)
    w_b = jnp.concatenate([cuts[k] for k in ("gl", "mq")], axis=1)
    row = lambda v: v.reshape(1, -1).astype(F32)
    return dict(
        w_a=w_a.astype(MXU_DT), w_b=w_b.astype(MXU_DT), sink=sink.astype(F32), conv_w=conv_w.astype(F32),
        w_mem_k=w_mem_k.astype(MXU_DT), w_mem_v=w_mem_v.astype(MXU_DT),
        wa=w_attn_o.astype(MXU_DT), wc=w_conv_out.astype(MXU_DT), wm=w_mem_o.astype(MXU_DT),
        w_out=w_out.astype(MXU_DT), ln1_g=row(ln1_g), ln1_b=row(ln1_b),
        wq_t=w_peer_q.T.astype(MXU_DT),
        keys=peer_keys.reshape(PEER_HEADS * 2, N_KEYS, PEER_HALF).astype(MXU_DT),
        u=peer_u.astype(MXU_DT), vt=_step_major_transpose(peer_v).astype(MXU_DT),
        ln2_g=row(ln2_g), ln2_b=row(ln2_b))


def _encode_group(x, mem, p):
    b, s, d = x.shape
    x2 = x.reshape(b * s, d)
    mem2 = mem.reshape(b * N_MEM, d)
    proj = _matmul(x2, p["w_a"], MXU_DT, PROJ_TM, PROJ_A_TN)
    proj_b = _matmul(x2, p["w_b"], MXU_DT, PROJ_TM, PROJ_B_TN)
    mem_k = _matmul(mem2, p["w_mem_k"], MXU_DT, 512, 512)
    mem_v = _matmul(mem2, p["w_mem_v"], MXU_DT, 512, 512)
    attn = _attention(proj, p["sink"], s)
    merged = _mixer(attn, proj, proj_b, mem_k, mem_v, p["conv_w"], p["wa"], p["wc"], p["wm"], s)
    h, ht, cm, r1, m1 = _ln1_route(x2, merged, p["w_out"], p["ln1_g"], p["ln1_b"], p["wq_t"], p["keys"])
    peer_t = _peer_dense(ht, p["u"], p["vt"], cm, r1, m1)
    return _ln2(h, peer_t, p["ln2_g"], p["ln2_b"]).reshape(b, s, d)


def kernel(x_prompt, x_sample, mem_prompt, mem_sample, w_in, sink, conv_w, w_mem_k, w_mem_v, w_attn_o,
           w_conv_out, w_mem_o, w_out, ln1_g, ln1_b, w_peer_q, peer_keys, peer_u, peer_v, ln2_g, ln2_b):
    assert w_in.shape[0] == DEPTH == 1
    p = _prepare(w_in[0], sink[0], conv_w[0], w_mem_k[0], w_mem_v[0], w_attn_o[0], w_conv_out[0],
                 w_mem_o[0], w_out[0], ln1_g[0], ln1_b[0], w_peer_q[0], peer_keys[0], peer_u[0],
                 peer_v[0], ln2_g[0], ln2_b[0])
    return (_encode_group(x_prompt, mem_prompt, p), _encode_group(x_sample, mem_sample, p))
```

```python
import functools
import math

import jax
import jax.numpy as jnp
from jax import lax
from jax.experimental import pallas as pl
from jax.experimental.pallas import tpu as pltpu

D_MODEL = 2048
N_Q_HEADS = 8
N_KV_HEADS = 2
GQA_GROUP = N_Q_HEADS // N_KV_HEADS
HEAD_DIM = 128
WINDOW = 128
ATTN_WIDTH = N_Q_HEADS * HEAD_DIM
KV_WIDTH = N_KV_HEADS * HEAD_DIM
CONV_WIDTH = 1024
N_MEM = 256
MEM_HEADS = 4
MEM_HEAD_DIM = 256
MEM_WIDTH = MEM_HEADS * MEM_HEAD_DIM
PEER_HEADS = 8
N_KEYS = 128
N_EXPERTS = N_KEYS * N_KEYS
PEER_HALF = 128
PEER_TOPK = 16
LN_EPS = 1e-5
DEPTH = 1
ALPHA = (2.0 * DEPTH) ** 0.25
GATE_WIDTH = 3 * D_MODEL

COL_Q = 0
COL_CH = COL_Q + ATTN_WIDTH
COL_CB = COL_CH + CONV_WIDTH
COL_CC = COL_CB + CONV_WIDTH
COL_K = COL_CC + CONV_WIDTH
COL_V = COL_K + KV_WIDTH
PROJ_A_WIDTH = COL_V + KV_WIDTH
COL_GL = 0
COL_MQ = COL_GL + GATE_WIDTH
PROJ_B_WIDTH = COL_MQ + MEM_WIDTH

MXU_DT = jnp.bfloat16
GATE_DT = jnp.bfloat16
VMEM_LIMIT = 56 * 1024 * 1024

PROJ_TM = 1024
PROJ_A_TN = PROJ_A_WIDTH // 3
PROJ_B_TN = PROJ_B_WIDTH // 4
ATTN_BLOCKS = 8
MIX_T = 512
ROUTE_T = 256
ROUTE_LANES = 128
PEER_C = 512
PEER_EB = 256
PEER_SUB = 8
LN2_T = 512

F32 = jnp.float32


def _cparams(*sem):
    return pltpu.CompilerParams(dimension_semantics=sem, vmem_limit_bytes=VMEM_LIMIT)


def _resident(shape):
    nd = len(shape)
    return pl.BlockSpec(shape, lambda *_: (0,) * nd, pipeline_mode=pl.Buffered(1))


def _layer_norm(x, g, b):
    mu = jnp.mean(x, axis=-1, keepdims=True)
    xc = x - mu
    var = jnp.mean(xc * xc, axis=-1, keepdims=True)
    return xc * lax.rsqrt(var + LN_EPS) * g + b


def _mm_kernel(x_ref, w_ref, o_ref, xb_ref):
    @pl.when(pl.program_id(1) == 0)
    def _():
        xb_ref[...] = x_ref[...].astype(xb_ref.dtype)

    o_ref[...] = jnp.dot(xb_ref[...], w_ref[...], preferred_element_type=F32).astype(o_ref.dtype)


def _matmul(x, w, out_dtype, tm, tn):
    m, k = x.shape
    n = w.shape[1]
    tm, tn = min(tm, m), min(tn, n)
    assert m % tm == 0 and n % tn == 0
    return pl.pallas_call(
        _mm_kernel,
        grid=(m // tm, n // tn),
        in_specs=[pl.BlockSpec((tm, k), lambda i, j: (i, 0)),
                  pl.BlockSpec((k, tn), lambda i, j: (0, j))],
        out_specs=pl.BlockSpec((tm, tn), lambda i, j: (i, j)),
        out_shape=jax.ShapeDtypeStruct((m, n), out_dtype),
        scratch_shapes=[pltpu.VMEM((tm, k), w.dtype)],
        compiler_params=_cparams("parallel", "arbitrary"),
        name="matmul",
    )(x, w)


def _attn_kernel(sink_ref, q_ref, kp_ref, km_ref, kn_ref, vp_ref, vm_ref, vn_ref, o_ref, *, tiles_per_seq):
    t = pl.program_id(0)
    pos = t % tiles_per_seq
    lo = jnp.where(pos == 0, WINDOW, 0)
    hi = jnp.where(pos == tiles_per_seq - 1, 2 * WINDOW, 3 * WINDOW)
    kall = jnp.concatenate([kp_ref[...], km_ref[...], kn_ref[...]], axis=0)
    vall = jnp.concatenate([vp_ref[...], vm_ref[...], vn_ref[...]], axis=0)
    qi = lax.broadcasted_iota(jnp.int32, (WINDOW, 3 * WINDOW), 0)
    si = lax.broadcasted_iota(jnp.int32, (WINDOW, 3 * WINDOW), 1)
    dist = jnp.abs(qi + WINDOW - si)
    band = dist <= WINDOW
    distf = dist.astype(F32)
    scale = 1.0 / math.sqrt(HEAD_DIM)
    n_blocks = q_ref.shape[0] // WINDOW
    for r in range(n_blocks):
        valid = band
        if r == 0:
            valid = valid & (si >= lo)
        if r == n_blocks - 1:
            valid = valid & (si < hi)
        rows = slice(r * WINDOW, (r + 1) * WINDOW)
        for kv in range(N_KV_HEADS):
            cols = slice(kv * HEAD_DIM, (kv + 1) * HEAD_DIM)
            kc = kall[r * WINDOW:(r + 3) * WINDOW, cols]
            vc = vall[r * WINDOW:(r + 3) * WINDOW, cols]
            heads = [kv * GQA_GROUP + g for g in range(GQA_GROUP)]
            q4 = jnp.concatenate([q_ref[rows, h * HEAD_DIM:(h + 1) * HEAD_DIM] for h in heads], axis=0)
            s4 = lax.dot_general(q4, kc, (((1,), (1,)), ((), ())), preferred_element_type=F32) * scale
            ps, inv = [], []
            for g, h in enumerate(heads):
                slope = 2.0 ** (-8.0 * (h + 1) / N_Q_HEADS)
                s = jnp.where(valid, s4[g * WINDOW:(g + 1) * WINDOW] - slope * distf, -jnp.inf)
                sk = sink_ref[h]
                m = jnp.maximum(jnp.max(s, axis=-1, keepdims=True), sk)
                p = jnp.exp(s - m)
                denom = jnp.sum(p, axis=-1, keepdims=True) + jnp.exp(sk - m)
                ps.append(p.astype(vc.dtype))
                inv.append(1.0 / denom)
            o4 = jnp.dot(jnp.concatenate(ps, axis=0), vc, preferred_element_type=F32)
            for g, h in enumerate(heads):
                o_ref[rows, h * HEAD_DIM:(h + 1) * HEAD_DIM] = (
                    o4[g * WINDOW:(g + 1) * WINDOW] * inv[g]).astype(o_ref.dtype)


def _attention(proj, sink, seq_len):
    n = proj.shape[0]
    ta = ATTN_BLOCKS * WINDOW
    assert seq_len % ta == 0 and n % seq_len == 0
    nblk = n // WINDOW
    kcol, vcol = COL_K // KV_WIDTH, COL_V // KV_WIDTH

    def halo(col, side):
        if side < 0:
            return pl.BlockSpec((WINDOW, KV_WIDTH), lambda t: (jnp.maximum(t * ATTN_BLOCKS - 1, 0), col))
        return pl.BlockSpec((WINDOW, KV_WIDTH), lambda t: (jnp.minimum((t + 1) * ATTN_BLOCKS, nblk - 1), col))

    def main(col):
        return pl.BlockSpec((ta, KV_WIDTH), lambda t: (t, col))

    return pl.pallas_call(
        functools.partial(_attn_kernel, tiles_per_seq=seq_len // ta),
        grid=(n // ta,),
        in_specs=[pl.BlockSpec(memory_space=pltpu.SMEM),
                  pl.BlockSpec((ta, ATTN_WIDTH), lambda t: (t, COL_Q // ATTN_WIDTH)),
                  halo(kcol, -1), main(kcol), halo(kcol, 1),
                  halo(vcol, -1), main(vcol), halo(vcol, 1)],
        out_specs=pl.BlockSpec((ta, ATTN_WIDTH), lambda t: (t, 0)),
        out_shape=jax.ShapeDtypeStruct((n, ATTN_WIDTH), proj.dtype),
        compiler_params=_cparams("parallel"),
        name="attention",
    )(sink, proj, proj, proj, proj, proj, proj, proj)


def _mixer_kernel(attn_ref, ch_ref, cb_ref, cc_ref, chp_ref, ccp_ref, chn_ref, ccn_ref,
                  g0_ref, g1_ref, g2_ref, mq_ref, mk_ref, mv_ref, convw_ref,
                  wa_ref, wc_ref, wm_ref, o_ref, *, tiles_per_seq):
    t = pl.program_id(0)
    pos = t % tiles_per_seq
    tt = ch_ref.shape[0]
    z = cc_ref[...].astype(F32) * ch_ref[...].astype(F32)
    zp = ccp_ref[7:8, :].astype(F32) * chp_ref[7:8, :].astype(F32)
    zn = ccn_ref[0:1, :].astype(F32) * chn_ref[0:1, :].astype(F32)
    zp = jnp.where(pos == 0, 0.0, zp)
    zn = jnp.where(pos == tiles_per_seq - 1, 0.0, zn)
    row = lax.broadcasted_iota(jnp.int32, z.shape, 0)
    z_up = jnp.where(row == 0, zp, pltpu.roll(z, 1, 0))
    z_dn = jnp.where(row == tt - 1, zn, pltpu.roll(z, tt - 1, 0))
    conv = z_up * convw_ref[0:1, :] + z * convw_ref[1:2, :] + z_dn * convw_ref[2:3, :]
    cv = (cb_ref[...].astype(F32) * conv).astype(wc_ref.dtype)
    mscale = 1.0 / math.sqrt(MEM_HEAD_DIM)
    mo = []
    for hm in range(MEM_HEADS):
        cols = slice(hm * MEM_HEAD_DIM, (hm + 1) * MEM_HEAD_DIM)
        s = lax.dot_general(mq_ref[:, cols], mk_ref[:, cols], (((1,), (1,)), ((), ())),
                            preferred_element_type=F32) * mscale
        p = jnp.exp(s - jnp.max(s, axis=-1, keepdims=True))
        inv = 1.0 / jnp.sum(p, axis=-1, keepdims=True)
        o = jnp.dot(p.astype(mv_ref.dtype), mv_ref[:, cols], preferred_element_type=F32) * inv
        mo.append(o.astype(wm_ref.dtype))
    memo = jnp.concatenate(mo, axis=-1)
    a_out = jnp.dot(attn_ref[...], wa_ref[...], preferred_element_type=F32)
    c_out = jnp.dot(cv, wc_ref[...], preferred_element_type=F32)
    m_out = jnp.dot(memo, wm_ref[...], preferred_element_type=F32)
    merged = (jax.nn.sigmoid(g0_ref[...].astype(F32)) * a_out
              + jax.nn.sigmoid(g1_ref[...].astype(F32)) * c_out
              + jax.nn.sigmoid(g2_ref[...].astype(F32)) * m_out)
    o_ref[...] = merged.astype(o_ref.dtype)


def _mixer(attn, proj, proj_b, mem_k, mem_v, conv_w, wa, wc, wm, seq_len):
    n = attn.shape[0]
    tt = MIX_T
    assert seq_len % tt == 0
    tiles_per_seq = seq_len // tt
    n8 = n // 8

    def col(c, width):
        return pl.BlockSpec((tt, width), lambda t: (t, c // width))

    def prev8(c):
        return pl.BlockSpec((8, CONV_WIDTH), lambda t: (jnp.maximum(t * (tt // 8) - 1, 0), c // CONV_WIDTH))

    def next8(c):
        return pl.BlockSpec((8, CONV_WIDTH), lambda t: (jnp.minimum((t + 1) * (tt // 8), n8 - 1), c // CONV_WIDTH))

    def mem_spec():
        return pl.BlockSpec((N_MEM, MEM_WIDTH), lambda t: (t // tiles_per_seq, 0))

    return pl.pallas_call(
        functools.partial(_mixer_kernel, tiles_per_seq=tiles_per_seq),
        grid=(n // tt,),
        in_specs=[pl.BlockSpec((tt, ATTN_WIDTH), lambda t: (t, 0)),
                  col(COL_CH, CONV_WIDTH), col(COL_CB, CONV_WIDTH), col(COL_CC, CONV_WIDTH),
                  prev8(COL_CH), prev8(COL_CC), next8(COL_CH), next8(COL_CC),
                  col(COL_GL, D_MODEL), col(COL_GL + D_MODEL, D_MODEL), col(COL_GL + 2 * D_MODEL, D_MODEL),
                  col(COL_MQ, MEM_WIDTH), mem_spec(), mem_spec(),
                  _resident(conv_w.shape), _resident(wa.shape), _resident(wc.shape), _resident(wm.shape)],
        out_specs=pl.BlockSpec((tt, D_MODEL), lambda t: (t, 0)),
        out_shape=jax.ShapeDtypeStruct((n, D_MODEL), attn.dtype),
        compiler_params=_cparams("parallel"),
        name="mixer",
    )(attn, proj, proj, proj, proj, proj, proj, proj, proj_b, proj_b, proj_b, proj_b,
      mem_k, mem_v, conv_w, wa, wc, wm)


def _sort_network(n):
    pairs, p = [], 1
    while p < n:
        k = p
        while k >= 1:
            for j in range(k % p, n - k, 2 * k):
                for i in range(min(k, n - j - k)):
                    if (i + j) // (2 * p) == (i + j + k) // (2 * p):
                        pairs.append((i + j, i + j + k))
            k //= 2
        p *= 2
    return pairs


def _compare_exchange(v, i, j):
    v[i], v[j] = jnp.maximum(v[i], v[j]), jnp.minimum(v[i], v[j])


def _top16_rows(v, n_valid):
    assert 2 * n_valid >= PEER_TOPK
    v = list(v)
    for i, j in _sort_network(PEER_TOPK):
        if j < n_valid:
            _compare_exchange(v, i, j)
    for shift in (4, 2, 1):
        other = [pltpu.roll(x, shift, 0) for x in v[:n_valid]]
        merged = []
        for a in range(PEER_TOPK):
            b = PEER_TOPK - 1 - a
            if a < n_valid and b < n_valid:
                merged.append(jnp.maximum(v[a], other[b]))
            else:
                merged.append(v[a] if a < n_valid else other[b])
        v, n_valid = merged, PEER_TOPK
        for d in (8, 4, 2, 1):
            for a in range(PEER_TOPK):
                if a & d == 0:
                    _compare_exchange(v, a, a + d)
    return v


def _rows_to_block(rows, sub):
    out = rows[0]
    for k in range(1, 8):
        out = jnp.where(sub == k, rows[k], out)
    return out


def _all_sublanes(x, op):
    for shift in (4, 2, 1):
        x = op(x, pltpu.roll(x, shift, 0))
    return x


def _first_true(pred, thr):
    assert len(thr) == PEER_TOPK == 16
    q1 = pred(thr[7])
    q2 = pred(jnp.where(q1, thr[3], thr[11]))
    q3 = pred(jnp.where(q1, jnp.where(q2, thr[1], thr[5]), jnp.where(q2, thr[9], thr[13])))
    low = jnp.where(q2, jnp.where(q3, thr[0], thr[2]), jnp.where(q3, thr[4], thr[6]))
    high = jnp.where(q2, jnp.where(q3, thr[8], thr[10]), jnp.where(q3, thr[12], thr[14]))
    q4 = pred(jnp.where(q1, low, high))
    k = (jnp.where(q1, 0.0, 8.0) + jnp.where(q2, 0.0, 4.0)
         + jnp.where(q3, 0.0, 2.0) + jnp.where(q4, 0.0, 1.0))
    return jnp.where(pred(thr[15]), k, float(PEER_TOPK))


def _route_column(s0, s1):
    n_slab = N_KEYS // 8
    slabs0 = [s0[8 * v:8 * v + 8] for v in range(n_slab)]
    slabs1 = [s1[8 * v:8 * v + 8] for v in range(n_slab)]
    t0 = _top16_rows(slabs0, n_slab)
    t1 = _top16_rows(slabs1, n_slab)
    sub = lax.broadcasted_iota(jnp.int32, slabs0[0].shape, 0)
    v0 = (_rows_to_block(t0[:8], sub), _rows_to_block(t0[8:], sub))
    v1_lo = _rows_to_block(t1[:8], sub)
    cand = [t0[0] + v1_lo, t0[0] + _rows_to_block(t1[8:], sub), t0[1] + v1_lo]
    for a in range(2, 8):
        cand.append(jnp.where(sub < PEER_TOPK // (a + 1), t0[a] + v1_lo, -jnp.inf))
    cand.append(v0[1] + t1[0])
    thr = _top16_rows(cand + [None] * (PEER_TOPK - len(cand)), len(cand))[PEER_TOPK - 1]
    e0 = [jnp.exp(x - t0[0]) for x in v0]
    zacc = [jnp.zeros_like(thr), jnp.zeros_like(thr)]
    lowest = []
    for b in range(PEER_TOPK):
        e1b = jnp.exp(t1[b] - t1[0])
        low = None
        for half in range(2):
            sel = (v0[half] + t1[b]) >= thr
            zacc[half] = zacc[half] + jnp.where(sel, e1b, 0.0)
            cur = jnp.where(sel, v0[half], jnp.inf)
            low = cur if low is None else jnp.minimum(low, cur)
        lowest.append(_all_sublanes(low, jnp.minimum))
    inv_z = 1.0 / _all_sublanes(e0[0] * zacc[0] + e0[1] * zacc[1], jnp.add)
    c0, m0, r1, m1 = [], [], [], []
    for v in range(n_slab):
        c0.append(_first_true(lambda thr: slabs0[v] < thr, lowest))
        r1.append(_first_true(lambda thr: slabs1[v] >= thr, t1))
        m0.append(jnp.exp(slabs0[v] - t0[0]) * inv_z)
        m1.append(jnp.exp(slabs1[v] - t1[0]))
    cat = lambda xs: jnp.concatenate(xs, axis=0)
    return cat(c0), cat(m0), cat(r1), cat(m1)


COUNT_MASK = 0xFF


def _pack_count_factor(count, factor):
    assert jnp.dtype(GATE_DT).itemsize == 2
    top = lax.bitcast_convert_type(factor.astype(GATE_DT).astype(F32), jnp.uint32)
    return top | count.astype(jnp.int32).astype(jnp.uint32)


def _ln1_route_kernel(x_ref, mg_ref, wo_ref, g_ref, b_ref, wq_ref, keys_ref,
                      h_ref, ht_ref, cm_ref, r1_ref, m1_ref, qt_scr):
    mix = jnp.dot(mg_ref[...], wo_ref[...], preferred_element_type=F32)
    h = _layer_norm(ALPHA * x_ref[...] + mix, g_ref[...], b_ref[...])
    h_ref[...] = h
    ht = h.T.astype(ht_ref.dtype)
    ht_ref[...] = ht
    qt_scr[...] = jnp.dot(wq_ref[...], ht, preferred_element_type=F32).astype(qt_scr.dtype)

    def route(hd, lanes):
        base = pl.multiple_of(hd * (2 * PEER_HALF), 2 * PEER_HALF)
        s0 = jnp.dot(keys_ref[2 * hd], qt_scr[pl.ds(base, PEER_HALF), lanes], preferred_element_type=F32)
        s1 = jnp.dot(keys_ref[2 * hd + 1], qt_scr[pl.ds(base + PEER_HALF, PEER_HALF), lanes],
                     preferred_element_type=F32)
        c0, m0, r1, m1 = _route_column(s0, s1)
        out_rows = pl.ds(pl.multiple_of(hd * N_KEYS, N_KEYS), N_KEYS)
        cm_ref[out_rows, lanes] = _pack_count_factor(c0, m0)
        r1_ref[out_rows, lanes] = r1.astype(r1_ref.dtype)
        m1_ref[out_rows, lanes] = m1.astype(m1_ref.dtype)

    def head(hd, carry):
        for col in range(x_ref.shape[0] // ROUTE_LANES):
            route(hd, slice(col * ROUTE_LANES, (col + 1) * ROUTE_LANES))
        return carry

    lax.fori_loop(0, PEER_HEADS, head, 0)


def _ln1_route(x, merged, w_out, ln_g, ln_b, wq_t, keys):
    n = x.shape[0]
    tt = ROUTE_T
    rows = PEER_HEADS * N_KEYS
    tok = lambda t: (t, 0)
    tok_t = lambda t: (0, t)
    return pl.pallas_call(
        _ln1_route_kernel,
        grid=(n // tt,),
        in_specs=[pl.BlockSpec((tt, D_MODEL), tok), pl.BlockSpec((tt, D_MODEL), tok),
                  _resident(w_out.shape), _resident(ln_g.shape), _resident(ln_b.shape),
                  _resident(wq_t.shape), _resident(keys.shape)],
        out_specs=[pl.BlockSpec((tt, D_MODEL), tok), pl.BlockSpec((D_MODEL, tt), tok_t),
                   pl.BlockSpec((rows, tt), tok_t), pl.BlockSpec((rows, tt), tok_t),
                   pl.BlockSpec((rows, tt), tok_t)],
        out_shape=[jax.ShapeDtypeStruct((n, D_MODEL), F32),
                   jax.ShapeDtypeStruct((D_MODEL, n), MXU_DT),
                   jax.ShapeDtypeStruct((rows, n), jnp.uint32),
                   jax.ShapeDtypeStruct((rows, n), GATE_DT), jax.ShapeDtypeStruct((rows, n), GATE_DT)],
        scratch_shapes=[pltpu.VMEM((PEER_HEADS * 2 * PEER_HALF, tt), MXU_DT)],
        compiler_params=_cparams("parallel"),
        name="ln1_route",
    )(x, merged, w_out, ln_g, ln_b, wq_t, keys)


def _peer_gate_block(block, pre, cm_ref, r1_ref, m1_ref):
    act = (0.5 * pre * (1.0 + lax.erf(pre * math.sqrt(0.5)))).astype(GATE_DT)
    per_block = pre.shape[0] // N_KEYS
    lanes = pre.shape[1]
    tiles = (N_KEYS // 16, 16, lanes)
    out = []
    for ii in range(per_block):
        i = block * per_block + ii
        gate = None
        for hd in range(PEER_HEADS):
            word = jnp.broadcast_to(cm_ref[pl.ds(hd * N_KEYS + i, 1), :], (16, lanes))
            m0 = lax.bitcast_convert_type(word, F32).astype(GATE_DT)
            c0 = (word & COUNT_MASK).astype(jnp.int32).astype(F32).astype(GATE_DT)
            r1 = r1_ref[hd * N_KEYS:(hd + 1) * N_KEYS, :].reshape(tiles)
            m1 = m1_ref[hd * N_KEYS:(hd + 1) * N_KEYS, :].reshape(tiles)
            term = jnp.where(r1 < c0[None], m1, jnp.zeros_like(m1)) * m0[None]
            gate = term if gate is None else gate + term
        gate = gate.reshape(N_KEYS, lanes)
        out.append((gate * act[ii * N_KEYS:(ii + 1) * N_KEYS, :]).astype(MXU_DT))
    return jnp.concatenate(out, axis=0)


def _peer_kernel(ht_ref, u_ref, vt_ref, cm_ref, r1_ref, m1_ref, o_ref):
    g = pl.program_id(1)
    eb = PEER_EB
    n_sub = u_ref.shape[0] // eb

    @pl.when(g == 0)
    def _():
        o_ref[...] = jnp.zeros_like(o_ref)

    routing = (cm_ref, r1_ref, m1_ref)
    pre = [jnp.dot(u_ref[k * eb:(k + 1) * eb, :], ht_ref[...], preferred_element_type=F32)
           for k in range(n_sub)]
    for k in range(n_sub):
        w = _peer_gate_block(g * n_sub + k, pre[k], *routing)
        o_ref[...] += jnp.dot(vt_ref[:, k * eb:(k + 1) * eb], w, preferred_element_type=F32)


def _peer_dense(ht, u, vt, cm, r1, m1):
    n = ht.shape[1]
    c, step = PEER_C, PEER_EB * PEER_SUB
    rows = PEER_HEADS * N_KEYS
    tok_t = lambda t, g: (0, t)
    tile = lambda r: pl.BlockSpec((r, c), tok_t)
    return pl.pallas_call(
        _peer_kernel,
        grid=(n // c, N_EXPERTS // step),
        in_specs=[tile(D_MODEL),
                  pl.BlockSpec((step, D_MODEL), lambda t, g: (g, 0)),
                  pl.BlockSpec((D_MODEL, step), lambda t, g: (0, g)),
                  tile(rows), tile(rows), tile(rows)],
        out_specs=tile(D_MODEL),
        out_shape=jax.ShapeDtypeStruct((D_MODEL, n), F32),
        compiler_params=_cparams("parallel", "arbitrary"),
        name="peer_dense",
    )(ht, u, vt, cm, r1, m1)


def _ln2_kernel(h_ref, pt_ref, g_ref, b_ref, o_ref):
    o_ref[...] = _layer_norm(ALPHA * h_ref[...] + pt_ref[...].T, g_ref[...], b_ref[...])


def _ln2(h, peer_t, ln_g, ln_b):
    n = h.shape[0]
    tt = LN2_T
    return pl.pallas_call(
        _ln2_kernel,
        grid=(n // tt,),
        in_specs=[pl.BlockSpec((tt, D_MODEL), lambda t: (t, 0)),
                  pl.BlockSpec((D_MODEL, tt), lambda t: (0, t)),
                  _resident(ln_g.shape), _resident(ln_b.shape)],
        out_specs=pl.BlockSpec((tt, D_MODEL), lambda t: (t, 0)),
        out_shape=jax.ShapeDtypeStruct((n, D_MODEL), F32),
        compiler_params=_cparams("parallel"),
        name="ln2",
    )(h, peer_t, ln_g, ln_b)


def _prepare(w_in, sink, conv_w, w_mem_k, w_mem_v, w_attn_o, w_conv_out, w_mem_o, w_out,
             ln1_g, ln1_b, w_peer_q, peer_keys, peer_u, peer_v, ln2_g, ln2_b):
    cuts, off = {}, 0
    for name, width in (("q", ATTN_WIDTH), ("k", KV_WIDTH), ("v", KV_WIDTH), ("ch", CONV_WIDTH),
                        ("cb", CONV_WIDTH), ("cc", CONV_WIDTH), ("mq", MEM_WIDTH), ("gl", GATE_WIDTH)):
        cuts[name] = w_in[:, off:off + width]
        off += width
    w_a = jnp.concatenate([cuts[k] for k in ("q", "ch", "cb", "cc", "k", "v")], axis=1)
    w_b = jnp.concatenate([cuts[k] for k in ("gl", "mq")], axis=1)
    row = lambda v: v.reshape(1, -1).astype(F32)
    return dict(
        w_a=w_a.astype(MXU_DT), w_b=w_b.astype(MXU_DT), sink=sink.astype(F32), conv_w=conv_w.astype(F32),
        w_mem_k=w_mem_k.astype(MXU_DT), w_mem_v=w_mem_v.astype(MXU_DT),
        wa=w_attn_o.astype(MXU_DT), wc=w_conv_out.astype(MXU_DT), wm=w_mem_o.astype(MXU_DT),
        w_out=w_out.astype(MXU_DT), ln1_g=row(ln1_g), ln1_b=row(ln1_b),
        wq_t=w_peer_q.astype(MXU_DT).T,
        keys=peer_keys.reshape(PEER_HEADS * 2, N_KEYS, PEER_HALF).astype(MXU_DT),
        u=peer_u.astype(MXU_DT), vt=peer_v.astype(MXU_DT).T, ln2_g=row(ln2_g), ln2_b=row(ln2_b))


def _encode_group(x, mem, p):
    b, s, d = x.shape
    x2 = x.reshape(b * s, d)
    mem2 = mem.reshape(b * N_MEM, d)
    proj = _matmul(x2, p["w_a"], MXU_DT, PROJ_TM, PROJ_A_TN)
    proj_b = _matmul(x2, p["w_b"], MXU_DT, PROJ_TM, PROJ_B_TN)
    mem_k = _matmul(mem2, p["w_mem_k"], MXU_DT, 512, 512)
    mem_v = _matmul(mem2, p["w_mem_v"], MXU_DT, 512, 512)
    attn = _attention(proj, p["sink"], s)
    merged = _mixer(attn, proj, proj_b, mem_k, mem_v, p["conv_w"], p["wa"], p["wc"], p["wm"], s)
    h, ht, cm, r1, m1 = _ln1_route(x2, merged, p["w_out"], p["ln1_g"], p["ln1_b"], p["wq_t"], p["keys"])
    peer_t = _peer_dense(ht, p["u"], p["vt"], cm, r1, m1)
    return _ln2(h, peer_t, p["ln2_g"], p["ln2_b"]).reshape(b, s, d)


def kernel(x_prompt, x_sample, mem_prompt, mem_sample, w_in, sink, conv_w, w_mem_k, w_mem_v, w_attn_o,
           w_conv_out, w_mem_o, w_out, ln1_g, ln1_b, w_peer_q, peer_keys, peer_u, peer_v, ln2_g, ln2_b):
    assert w_in.shape[0] == DEPTH == 1
    p = _prepare(w_in[0], sink[0], conv_w[0], w_mem_k[0], w_mem_v[0], w_attn_o[0], w_conv_out[0],
                 w_mem_o[0], w_out[0], ln1_g[0], ln1_b[0], w_peer_q[0], peer_keys[0], peer_u[0],
                 peer_v[0], ln2_g[0], ln2_b[0])
    return (_encode_group(x_prompt, mem_prompt, p), _encode_group(x_sample, mem_sample, p))
```

```python
import functools
import math

import jax
import jax.numpy as jnp
from jax import lax
from jax.experimental import pallas as pl
from jax.experimental.pallas import tpu as pltpu

D_MODEL = 2048
N_Q_HEADS = 8
N_KV_HEADS = 2
GQA_GROUP = N_Q_HEADS // N_KV_HEADS
HEAD_DIM = 128
WINDOW = 128
ATTN_WIDTH = N_Q_HEADS * HEAD_DIM
KV_WIDTH = N_KV_HEADS * HEAD_DIM
CONV_WIDTH = 1024
N_MEM = 256
MEM_HEADS = 4
MEM_HEAD_DIM = 256
MEM_WIDTH = MEM_HEADS * MEM_HEAD_DIM
PEER_HEADS = 8
N_KEYS = 128
N_EXPERTS = N_KEYS * N_KEYS
PEER_HALF = 128
PEER_TOPK = 16
LN_EPS = 1e-5
DEPTH = 1
ALPHA = (2.0 * DEPTH) ** 0.25
GATE_WIDTH = 3 * D_MODEL

COL_Q = 0
COL_K = COL_Q + ATTN_WIDTH
COL_V = COL_K + KV_WIDTH
COL_CH = COL_V + KV_WIDTH
COL_CB = COL_CH + CONV_WIDTH
COL_CC = COL_CB + CONV_WIDTH
PROJ_A_WIDTH = COL_CC + CONV_WIDTH
COL_MQ = 0
COL_GL = COL_MQ + MEM_WIDTH
PROJ_B_WIDTH = COL_GL + GATE_WIDTH

MXU_DT = jnp.bfloat16
GATE_DT = jnp.bfloat16
VMEM_LIMIT = 56 * 1024 * 1024

PROJ_TM = 1024
PROJ_A_TN = PROJ_A_WIDTH // 3
PROJ_B_TN = PROJ_B_WIDTH // 4
ATTN_BLOCKS = 8
MIX_T = 512
ROUTE_T = 256
ROUTE_LANES = 128
PEER_C = 512
PEER_EB = 256
PEER_SUB = 8
LN2_T = 512
TRANSPOSE_ROWS = 512

F32 = jnp.float32


def _cparams(*sem):
    return pltpu.CompilerParams(dimension_semantics=sem, vmem_limit_bytes=VMEM_LIMIT)


def _resident(shape):
    nd = len(shape)
    return pl.BlockSpec(shape, lambda *_: (0,) * nd, pipeline_mode=pl.Buffered(1))


def _layer_norm(x, g, b):
    mu = jnp.mean(x, axis=-1, keepdims=True)
    xc = x - mu
    var = jnp.mean(xc * xc, axis=-1, keepdims=True)
    return xc * lax.rsqrt(var + LN_EPS) * g + b


def _mm_kernel(x_ref, w_ref, o_ref, xb_ref):
    @pl.when(pl.program_id(1) == 0)
    def _():
        xb_ref[...] = x_ref[...].astype(xb_ref.dtype)

    o_ref[...] = jnp.dot(xb_ref[...], w_ref[...], preferred_element_type=F32).astype(o_ref.dtype)


def _matmul(x, w, out_dtype, tm, tn):
    m, k = x.shape
    n = w.shape[1]
    tm, tn = min(tm, m), min(tn, n)
    assert m % tm == 0 and n % tn == 0
    return pl.pallas_call(
        _mm_kernel,
        grid=(m // tm, n // tn),
        in_specs=[pl.BlockSpec((tm, k), lambda i, j: (i, 0)),
                  pl.BlockSpec((k, tn), lambda i, j: (0, j))],
        out_specs=pl.BlockSpec((tm, tn), lambda i, j: (i, j)),
        out_shape=jax.ShapeDtypeStruct((m, n), out_dtype),
        scratch_shapes=[pltpu.VMEM((tm, k), w.dtype)],
        compiler_params=_cparams("parallel", "arbitrary"),
        name="matmul",
    )(x, w)


def _attn_kernel(sink_ref, q_ref, kp_ref, km_ref, kn_ref, vp_ref, vm_ref, vn_ref, o_ref, *, tiles_per_seq):
    t = pl.program_id(0)
    pos = t % tiles_per_seq
    lo = jnp.where(pos == 0, WINDOW, 0)
    hi = jnp.where(pos == tiles_per_seq - 1, 2 * WINDOW, 3 * WINDOW)
    kall = jnp.concatenate([kp_ref[...], km_ref[...], kn_ref[...]], axis=0)
    vall = jnp.concatenate([vp_ref[...], vm_ref[...], vn_ref[...]], axis=0)
    qi = lax.broadcasted_iota(jnp.int32, (WINDOW, 3 * WINDOW), 0)
    si = lax.broadcasted_iota(jnp.int32, (WINDOW, 3 * WINDOW), 1)
    dist = jnp.abs(qi + WINDOW - si)
    band = dist <= WINDOW
    distf = dist.astype(F32)
    scale = 1.0 / math.sqrt(HEAD_DIM)
    n_blocks = q_ref.shape[0] // WINDOW
    for r in range(n_blocks):
        valid = band
        if r == 0:
            valid = valid & (si >= lo)
        if r == n_blocks - 1:
            valid = valid & (si < hi)
        rows = slice(r * WINDOW, (r + 1) * WINDOW)
        for kv in range(N_KV_HEADS):
            cols = slice(kv * HEAD_DIM, (kv + 1) * HEAD_DIM)
            kc = kall[r * WINDOW:(r + 3) * WINDOW, cols]
            vc = vall[r * WINDOW:(r + 3) * WINDOW, cols]
            heads = [kv * GQA_GROUP + g for g in range(GQA_GROUP)]
            q4 = jnp.concatenate([q_ref[rows, h * HEAD_DIM:(h + 1) * HEAD_DIM] for h in heads], axis=0)
            s4 = lax.dot_general(q4, kc, (((1,), (1,)), ((), ())), preferred_element_type=F32) * scale
            ps, inv = [], []
            for g, h in enumerate(heads):
                slope = 2.0 ** (-8.0 * (h + 1) / N_Q_HEADS)
                s = jnp.where(valid, s4[g * WINDOW:(g + 1) * WINDOW] - slope * distf, -jnp.inf)
                sk = sink_ref[h]
                m = jnp.maximum(jnp.max(s, axis=-1, keepdims=True), sk)
                p = jnp.exp(s - m)
                denom = jnp.sum(p, axis=-1, keepdims=True) + jnp.exp(sk - m)
                ps.append(p.astype(vc.dtype))
                inv.append(1.0 / denom)
            o4 = jnp.dot(jnp.concatenate(ps, axis=0), vc, preferred_element_type=F32)
            for g, h in enumerate(heads):
                o_ref[rows, h * HEAD_DIM:(h + 1) * HEAD_DIM] = (
                    o4[g * WINDOW:(g + 1) * WINDOW] * inv[g]).astype(o_ref.dtype)


def _attention(proj, sink, seq_len):
    n = proj.shape[0]
    ta = ATTN_BLOCKS * WINDOW
    assert seq_len % ta == 0 and n % seq_len == 0
    nblk = n // WINDOW
    kcol, vcol = COL_K // KV_WIDTH, COL_V // KV_WIDTH

    def halo(col, side):
        if side < 0:
            return pl.BlockSpec((WINDOW, KV_WIDTH), lambda t: (jnp.maximum(t * ATTN_BLOCKS - 1, 0), col))
        return pl.BlockSpec((WINDOW, KV_WIDTH), lambda t: (jnp.minimum((t + 1) * ATTN_BLOCKS, nblk - 1), col))

    def main(col):
        return pl.BlockSpec((ta, KV_WIDTH), lambda t: (t, col))

    return pl.pallas_call(
        functools.partial(_attn_kernel, tiles_per_seq=seq_len // ta),
        grid=(n // ta,),
        in_specs=[pl.BlockSpec(memory_space=pltpu.SMEM),
                  pl.BlockSpec((ta, ATTN_WIDTH), lambda t: (t, COL_Q // ATTN_WIDTH)),
                  halo(kcol, -1), main(kcol), halo(kcol, 1),
                  halo(vcol, -1), main(vcol), halo(vcol, 1)],
        out_specs=pl.BlockSpec((ta, ATTN_WIDTH), lambda t: (t, 0)),
        out_shape=jax.ShapeDtypeStruct((n, ATTN_WIDTH), proj.dtype),
        compiler_params=_cparams("parallel"),
        name="attention",
    )(sink, proj, proj, proj, proj, proj, proj, proj)


def _mixer_kernel(attn_ref, ch0, ch1, cb0, cb1, cc0, cc1, chp0, chp1, ccp0, ccp1, chn0, chn1, ccn0, ccn1,
                  g00, g01, g10, g11, g20, g21, mq_ref, mk_ref, mv_ref, convw_ref,
                  wa_ref, wc_ref, wm_ref, o_ref, *, tiles_per_seq):
    t = pl.program_id(0)
    pos = t % tiles_per_seq
    tt = ch0.shape[0]

    def wide(lo, hi, rows=slice(None)):
        return jnp.concatenate([lo[rows, :], hi[rows, :]], axis=-1).astype(F32)

    z = wide(cc0, cc1) * wide(ch0, ch1)
    zp = wide(ccp0, ccp1, slice(7, 8)) * wide(chp0, chp1, slice(7, 8))
    zn = wide(ccn0, ccn1, slice(0, 1)) * wide(chn0, chn1, slice(0, 1))
    zp = jnp.where(pos == 0, 0.0, zp)
    zn = jnp.where(pos == tiles_per_seq - 1, 0.0, zn)
    row = lax.broadcasted_iota(jnp.int32, z.shape, 0)
    z_up = jnp.where(row == 0, zp, pltpu.roll(z, 1, 0))
    z_dn = jnp.where(row == tt - 1, zn, pltpu.roll(z, tt - 1, 0))
    conv = z_up * convw_ref[0:1, :] + z * convw_ref[1:2, :] + z_dn * convw_ref[2:3, :]
    cv = (wide(cb0, cb1) * conv).astype(wc_ref.dtype)
    mscale = 1.0 / math.sqrt(MEM_HEAD_DIM)
    mo = []
    for hm in range(MEM_HEADS):
        cols = slice(hm * MEM_HEAD_DIM, (hm + 1) * MEM_HEAD_DIM)
        s = lax.dot_general(mq_ref[:, cols], mk_ref[:, cols], (((1,), (1,)), ((), ())),
                            preferred_element_type=F32) * mscale
        p = jnp.exp(s - jnp.max(s, axis=-1, keepdims=True))
        inv = 1.0 / jnp.sum(p, axis=-1, keepdims=True)
        o = jnp.dot(p.astype(mv_ref.dtype), mv_ref[:, cols], preferred_element_type=F32) * inv
        mo.append(o.astype(wm_ref.dtype))
    memo = jnp.concatenate(mo, axis=-1)
    a_out = jnp.dot(attn_ref[...], wa_ref[...], preferred_element_type=F32)
    c_out = jnp.dot(cv, wc_ref[...], preferred_element_type=F32)
    m_out = jnp.dot(memo, wm_ref[...], preferred_element_type=F32)
    merged = (jax.nn.sigmoid(wide(g00, g01)) * a_out
              + jax.nn.sigmoid(wide(g10, g11)) * c_out
              + jax.nn.sigmoid(wide(g20, g21)) * m_out)
    o_ref[...] = merged.astype(o_ref.dtype)


def _mixer(attn, proj, proj_b, mem_k, mem_v, conv_w, wa, wc, wm, seq_len):
    n = attn.shape[0]
    tt = MIX_T
    assert seq_len % tt == 0
    tiles_per_seq = seq_len // tt
    n8 = n // 8

    def halves(make, start, width):
        half = width // 2
        assert start % half == 0
        return [make(start // half + k, half) for k in range(2)]

    def col(block, width):
        return pl.BlockSpec((tt, width), lambda t: (t, block))

    def prev8(block, width):
        return pl.BlockSpec((8, width), lambda t: (jnp.maximum(t * (tt // 8) - 1, 0), block))

    def next8(block, width):
        return pl.BlockSpec((8, width), lambda t: (jnp.minimum((t + 1) * (tt // 8), n8 - 1), block))

    def mem_spec():
        return pl.BlockSpec((N_MEM, MEM_WIDTH), lambda t: (t // tiles_per_seq, 0))

    conv_specs = [s for start in (COL_CH, COL_CB, COL_CC) for s in halves(col, start, CONV_WIDTH)]
    halo_specs = [s for make, start in ((prev8, COL_CH), (prev8, COL_CC), (next8, COL_CH), (next8, COL_CC))
                  for s in halves(make, start, CONV_WIDTH)]
    gate_specs = [s for k in range(3) for s in halves(col, COL_GL + k * D_MODEL, D_MODEL)]
    assert COL_MQ % MEM_WIDTH == 0
    return pl.pallas_call(
        functools.partial(_mixer_kernel, tiles_per_seq=tiles_per_seq),
        grid=(n // tt,),
        in_specs=[pl.BlockSpec((tt, ATTN_WIDTH), lambda t: (t, 0)), *conv_specs, *halo_specs, *gate_specs,
                  col(COL_MQ // MEM_WIDTH, MEM_WIDTH), mem_spec(), mem_spec(),
                  _resident(conv_w.shape), _resident(wa.shape), _resident(wc.shape), _resident(wm.shape)],
        out_specs=pl.BlockSpec((tt, D_MODEL), lambda t: (t, 0)),
        out_shape=jax.ShapeDtypeStruct((n, D_MODEL), attn.dtype),
        compiler_params=_cparams("parallel"),
        name="mixer",
    )(attn, *[proj] * (len(conv_specs) + len(halo_specs)), *[proj_b] * (len(gate_specs) + 1),
      mem_k, mem_v, conv_w, wa, wc, wm)


def _sort_network(n):
    pairs, p = [], 1
    while p < n:
        k = p
        while k >= 1:
            for j in range(k % p, n - k, 2 * k):
                for i in range(min(k, n - j - k)):
                    if (i + j) // (2 * p) == (i + j + k) // (2 * p):
                        pairs.append((i + j, i + j + k))
            k //= 2
        p *= 2
    return pairs


def _compare_exchange(v, i, j):
    v[i], v[j] = jnp.maximum(v[i], v[j]), jnp.minimum(v[i], v[j])


def _network_sort(v, n_valid):
    v = list(v)
    for i, j in _sort_network(PEER_TOPK):
        if j < n_valid:
            _compare_exchange(v, i, j)
    return v


def _merge_step(v, n_valid, shift):
    assert 2 * n_valid >= PEER_TOPK
    other = [pltpu.roll(x, shift, 0) for x in v[:n_valid]]
    merged = []
    for a in range(PEER_TOPK):
        b = PEER_TOPK - 1 - a
        if a < n_valid and b < n_valid:
            merged.append(jnp.maximum(v[a], other[b]))
        else:
            merged.append(v[a] if a < n_valid else other[b])
    for d in (8, 4, 2, 1):
        for a in range(PEER_TOPK):
            if a & d == 0:
                _compare_exchange(merged, a, a + d)
    return merged


def _top16_rows_pair(va, vb, n_valid, sub):
    va = _merge_step(_network_sort(va, n_valid), n_valid, 4)
    vb = _merge_step(_network_sort(vb, n_valid), n_valid, 4)
    x = [jnp.where(sub < 4, a, b) for a, b in zip(va, vb)]
    x = _merge_step(x, PEER_TOPK, 2)
    x = _merge_step(x, PEER_TOPK, 1)
    spread = lambda v, k: jnp.broadcast_to(v[k:k + 1], v.shape)
    return [spread(v, 3) for v in x], [spread(v, 7) for v in x]


def _rows_to_block(rows, sub):
    out = rows[0]
    for k in range(1, 8):
        out = jnp.where(sub == k, rows[k], out)
    return out


def _all_sublanes(x, op):
    for shift in (4, 2, 1):
        x = op(x, pltpu.roll(x, shift, 0))
    return x


def _first_true(pred, thr):
    assert len(thr) == PEER_TOPK == 16
    q1 = pred(thr[7])
    q2 = pred(jnp.where(q1, thr[3], thr[11]))
    q3 = pred(jnp.where(q1, jnp.where(q2, thr[1], thr[5]), jnp.where(q2, thr[9], thr[13])))
    low = jnp.where(q2, jnp.where(q3, thr[0], thr[2]), jnp.where(q3, thr[4], thr[6]))
    high = jnp.where(q2, jnp.where(q3, thr[8], thr[10]), jnp.where(q3, thr[12], thr[14]))
    q4 = pred(jnp.where(q1, low, high))
    k = (jnp.where(q1, 0.0, 8.0) + jnp.where(q2, 0.0, 4.0)
         + jnp.where(q3, 0.0, 2.0) + jnp.where(q4, 0.0, 1.0))
    return jnp.where(pred(thr[15]), k, float(PEER_TOPK))


def _slabs(s):
    return [s[8 * v:8 * v + 8] for v in range(s.shape[0] // 8)]


def _pair_sum_candidates(t0, t1, sub):
    v0 = (_rows_to_block(t0[:8], sub), _rows_to_block(t0[8:], sub))
    v1_lo = _rows_to_block(t1[:8], sub)
    cand = [t0[0] + v1_lo, t0[0] + _rows_to_block(t1[8:], sub), t0[1] + v1_lo]
    for a in range(2, 8):
        cand.append(jnp.where(sub < PEER_TOPK // (a + 1), t0[a] + v1_lo, -jnp.inf))
    cand.append(v0[1] + t1[0])
    return cand, v0


def _route_finish(slabs0, slabs1, t0, t1, v0, thr):
    e0 = [jnp.exp(x - t0[0]) for x in v0]
    zacc = [jnp.zeros_like(thr), jnp.zeros_like(thr)]
    lowest = []
    for b in range(PEER_TOPK):
        e1b = jnp.exp(t1[b] - t1[0])
        low = None
        for half in range(2):
            sel = (v0[half] + t1[b]) >= thr
            zacc[half] = zacc[half] + jnp.where(sel, e1b, 0.0)
            cur = jnp.where(sel, v0[half], jnp.inf)
            low = cur if low is None else jnp.minimum(low, cur)
        lowest.append(_all_sublanes(low, jnp.minimum))
    inv_z = 1.0 / _all_sublanes(e0[0] * zacc[0] + e0[1] * zacc[1], jnp.add)
    c0, m0, r1, m1 = [], [], [], []
    for v in range(len(slabs0)):
        c0.append(_first_true(lambda thr: slabs0[v] < thr, lowest))
        r1.append(_first_true(lambda thr: slabs1[v] >= thr, t1))
        m0.append(jnp.exp(slabs0[v] - t0[0]) * inv_z)
        m1.append(jnp.exp(slabs1[v] - t1[0]))
    cat = lambda xs: jnp.concatenate(xs, axis=0)
    return cat(c0), cat(m0), cat(r1), cat(m1)


def _route_column_pair(scores_a, scores_b):
    (a0, a1), (b0, b1) = [tuple(_slabs(s) for s in col) for col in (scores_a, scores_b)]
    sub = lax.broadcasted_iota(jnp.int32, a0[0].shape, 0)
    n_slab = len(a0)
    ta0, tb0 = _top16_rows_pair(a0, b0, n_slab, sub)
    ta1, tb1 = _top16_rows_pair(a1, b1, n_slab, sub)
    cand_a, va0 = _pair_sum_candidates(ta0, ta1, sub)
    cand_b, vb0 = _pair_sum_candidates(tb0, tb1, sub)
    pad = [None] * (PEER_TOPK - len(cand_a))
    sums_a, sums_b = _top16_rows_pair(cand_a + pad, cand_b + pad, len(cand_a), sub)
    return (_route_finish(a0, a1, ta0, ta1, va0, sums_a[PEER_TOPK - 1]),
            _route_finish(b0, b1, tb0, tb1, vb0, sums_b[PEER_TOPK - 1]))


COUNT_MASK = 0xFF


def _pack_count_factor(count, factor):
    assert jnp.dtype(GATE_DT).itemsize == 2
    top = lax.bitcast_convert_type(factor.astype(GATE_DT).astype(F32), jnp.uint32)
    return top | count.astype(jnp.int32).astype(jnp.uint32)


def _ln1_route_kernel(x_ref, mg_ref, wo_ref, g_ref, b_ref, wq_ref, keys_ref,
                      h_ref, ht_ref, cm_ref, r1_ref, m1_ref, qt_scr):
    mix = jnp.dot(mg_ref[...], wo_ref[...], preferred_element_type=F32)
    h = _layer_norm(ALPHA * x_ref[...] + mix, g_ref[...], b_ref[...])
    h_ref[...] = h
    ht = h.T.astype(ht_ref.dtype)
    ht_ref[...] = ht
    qt_scr[...] = jnp.dot(wq_ref[...], ht, preferred_element_type=F32).astype(qt_scr.dtype)

    def scores(hd, lanes):
        base = pl.multiple_of(hd * (2 * PEER_HALF), 2 * PEER_HALF)
        return (jnp.dot(keys_ref[2 * hd], qt_scr[pl.ds(base, PEER_HALF), lanes], preferred_element_type=F32),
                jnp.dot(keys_ref[2 * hd + 1], qt_scr[pl.ds(base + PEER_HALF, PEER_HALF), lanes],
                        preferred_element_type=F32))

    def head(hd, carry):
        out_rows = pl.ds(pl.multiple_of(hd * N_KEYS, N_KEYS), N_KEYS)
        for pair in range(x_ref.shape[0] // (2 * ROUTE_LANES)):
            cols = [slice((2 * pair + k) * ROUTE_LANES, (2 * pair + k + 1) * ROUTE_LANES) for k in range(2)]
            routed = _route_column_pair(scores(hd, cols[0]), scores(hd, cols[1]))
            for lanes, (c0, m0, r1, m1) in zip(cols, routed):
                cm_ref[out_rows, lanes] = _pack_count_factor(c0, m0)
                r1_ref[out_rows, lanes] = r1.astype(r1_ref.dtype)
                m1_ref[out_rows, lanes] = m1.astype(m1_ref.dtype)
        return carry

    lax.fori_loop(0, PEER_HEADS, head, 0)


def _ln1_route(x, merged, w_out, ln_g, ln_b, wq_t, keys):
    n = x.shape[0]
    tt = ROUTE_T
    rows = PEER_HEADS * N_KEYS
    tok = lambda t: (t, 0)
    tok_t = lambda t: (0, t)
    return pl.pallas_call(
        _ln1_route_kernel,
        grid=(n // tt,),
        in_specs=[pl.BlockSpec((tt, D_MODEL), tok), pl.BlockSpec((tt, D_MODEL), tok),
                  _resident(w_out.shape), _resident(ln_g.shape), _resident(ln_b.shape),
                  _resident(wq_t.shape), _resident(keys.shape)],
        out_specs=[pl.BlockSpec((tt, D_MODEL), tok), pl.BlockSpec((D_MODEL, tt), tok_t),
                   pl.BlockSpec((rows, tt), tok_t), pl.BlockSpec((rows, tt), tok_t),
                   pl.BlockSpec((rows, tt), tok_t)],
        out_shape=[jax.ShapeDtypeStruct((n, D_MODEL), F32),
                   jax.ShapeDtypeStruct((D_MODEL, n), MXU_DT),
                   jax.ShapeDtypeStruct((rows, n), jnp.uint32),
                   jax.ShapeDtypeStruct((rows, n), GATE_DT), jax.ShapeDtypeStruct((rows, n), GATE_DT)],
        scratch_shapes=[pltpu.VMEM((PEER_HEADS * 2 * PEER_HALF, tt), MXU_DT)],
        compiler_params=_cparams("parallel"),
        name="ln1_route",
    )(x, merged, w_out, ln_g, ln_b, wq_t, keys)


def _peer_gate_block(block, pre, cm_ref, r1_ref, m1_ref):
    act = (0.5 * pre * (1.0 + lax.erf(pre * math.sqrt(0.5)))).astype(GATE_DT)
    per_block = pre.shape[0] // N_KEYS
    lanes = pre.shape[1]
    tiles = (N_KEYS // 16, 16, lanes)
    out = []
    for ii in range(per_block):
        i = block * per_block + ii
        gate = None
        for hd in range(PEER_HEADS):
            word = jnp.broadcast_to(cm_ref[pl.ds(hd * N_KEYS + i, 1), :], (16, lanes))
            m0 = lax.bitcast_convert_type(word, F32).astype(GATE_DT)
            c0 = (word & COUNT_MASK).astype(jnp.int32).astype(F32).astype(GATE_DT)
            r1 = r1_ref[hd * N_KEYS:(hd + 1) * N_KEYS, :].reshape(tiles)
            m1 = m1_ref[hd * N_KEYS:(hd + 1) * N_KEYS, :].reshape(tiles)
            term = jnp.where(r1 < c0[None], m1, jnp.zeros_like(m1)) * m0[None]
            gate = term if gate is None else gate + term
        gate = gate.reshape(N_KEYS, lanes)
        out.append((gate * act[ii * N_KEYS:(ii + 1) * N_KEYS, :]).astype(MXU_DT))
    return jnp.concatenate(out, axis=0)


def _peer_kernel(ht_ref, u_ref, vt_ref, cm_ref, r1_ref, m1_ref, o_ref):
    g = pl.program_id(1)
    eb = PEER_EB
    n_sub = u_ref.shape[0] // eb

    @pl.when(g == 0)
    def _():
        o_ref[...] = jnp.zeros_like(o_ref)

    routing = (cm_ref, r1_ref, m1_ref)
    pre = [jnp.dot(u_ref[k * eb:(k + 1) * eb, :], ht_ref[...], preferred_element_type=F32)
           for k in range(n_sub)]
    for k in range(n_sub):
        w = _peer_gate_block(g * n_sub + k, pre[k], *routing)
        o_ref[...] += jnp.dot(vt_ref[:, k * eb:(k + 1) * eb], w, preferred_element_type=F32)


def _peer_dense(ht, u, vt, cm, r1, m1):
    n = ht.shape[1]
    c, step = PEER_C, PEER_EB * PEER_SUB
    rows = PEER_HEADS * N_KEYS
    tok_t = lambda t, g: (0, t)
    tile = lambda r: pl.BlockSpec((r, c), tok_t)
    return pl.pallas_call(
        _peer_kernel,
        grid=(n // c, N_EXPERTS // step),
        in_specs=[tile(D_MODEL),
                  pl.BlockSpec((step, D_MODEL), lambda t, g: (g, 0)),
                  pl.BlockSpec((D_MODEL, step), lambda t, g: (0, g)),
                  tile(rows), tile(rows), tile(rows)],
        out_specs=tile(D_MODEL),
        out_shape=jax.ShapeDtypeStruct((D_MODEL, n), F32),
        compiler_params=_cparams("parallel", "arbitrary"),
        name="peer_dense",
    )(ht, u, vt, cm, r1, m1)


def _transpose_cast_kernel(x_ref, o_ref):
    o_ref[...] = x_ref[...].T.astype(o_ref.dtype)


def _transpose_cast(table, dtype):
    rows, cols = table.shape
    tr = TRANSPOSE_ROWS
    assert rows % tr == 0
    return pl.pallas_call(
        _transpose_cast_kernel,
        grid=(rows // tr,),
        in_specs=[pl.BlockSpec((tr, cols), lambda i: (i, 0))],
        out_specs=pl.BlockSpec((cols, tr), lambda i: (0, i)),
        out_shape=jax.ShapeDtypeStruct((cols, rows), dtype),
        compiler_params=_cparams("parallel"),
        name="transpose_cast",
    )(table)


def _ln2_kernel(h_ref, pt_ref, g_ref, b_ref, o_ref):
    o_ref[...] = _layer_norm(ALPHA * h_ref[...] + pt_ref[...].T, g_ref[...], b_ref[...])


def _ln2(h, peer_t, ln_g, ln_b):
    n = h.shape[0]
    tt = LN2_T
    return pl.pallas_call(
        _ln2_kernel,
        grid=(n // tt,),
        in_specs=[pl.BlockSpec((tt, D_MODEL), lambda t: (t, 0)),
                  pl.BlockSpec((D_MODEL, tt), lambda t: (0, t)),
                  _resident(ln_g.shape), _resident(ln_b.shape)],
        out_specs=pl.BlockSpec((tt, D_MODEL), lambda t: (t, 0)),
        out_shape=jax.ShapeDtypeStruct((n, D_MODEL), F32),
        compiler_params=_cparams("parallel"),
        name="ln2",
    )(h, peer_t, ln_g, ln_b)


def _prepare(w_in, sink, conv_w, w_mem_k, w_mem_v, w_attn_o, w_conv_out, w_mem_o, w_out,
             ln1_g, ln1_b, w_peer_q, peer_keys, peer_u, peer_v, ln2_g, ln2_b):
    assert w_in.shape[1] == PROJ_A_WIDTH + PROJ_B_WIDTH
    w_in = w_in.astype(MXU_DT)
    row = lambda v: v.reshape(1, -1).astype(F32)
    return dict(
        w_a=w_in[:, :PROJ_A_WIDTH], w_b=w_in[:, PROJ_A_WIDTH:], sink=sink.astype(F32), conv_w=conv_w.astype(F32),
        w_mem_k=w_mem_k.astype(MXU_DT), w_mem_v=w_mem_v.astype(MXU_DT),
        wa=w_attn_o.astype(MXU_DT), wc=w_conv_out.astype(MXU_DT), wm=w_mem_o.astype(MXU_DT),
        w_out=w_out.astype(MXU_DT), ln1_g=row(ln1_g), ln1_b=row(ln1_b),
        wq_t=w_peer_q.astype(MXU_DT).T,
        keys=peer_keys.reshape(PEER_HEADS * 2, N_KEYS, PEER_HALF).astype(MXU_DT),
        u=peer_u.astype(MXU_DT), vt=_transpose_cast(peer_v, MXU_DT), ln2_g=row(ln2_g), ln2_b=row(ln2_b))


def _encode_group(x, mem, p):
    b, s, d = x.shape
    x2 = x.reshape(b * s, d)
    mem2 = mem.reshape(b * N_MEM, d)
    proj = _matmul(x2, p["w_a"], MXU_DT, PROJ_TM, PROJ_A_TN)
    proj_b = _matmul(x2, p["w_b"], MXU_DT, PROJ_TM, PROJ_B_TN)
    mem_k = _matmul(mem2, p["w_mem_k"], MXU_DT, 512, 512)
    mem_v = _matmul(mem2, p["w_mem_v"], MXU_DT, 512, 512)
    attn = _attention(proj, p["sink"], s)
    merged = _mixer(attn, proj, proj_b, mem_k, mem_v, p["conv_w"], p["wa"], p["wc"], p["wm"], s)
    h, ht, cm, r1, m1 = _ln1_route(x2, merged, p["w_out"], p["ln1_g"], p["ln1_b"], p["wq_t"], p["keys"])
    peer_t = _peer_dense(ht, p["u"], p["vt"], cm, r1, m1)
    return _ln2(h, peer_t, p["ln2_g"], p["ln2_b"]).reshape(b, s, d)


def kernel(x_prompt, x_sample, mem_prompt, mem_sample, w_in, sink, conv_w, w_mem_k, w_mem_v, w_attn_o,
           w_conv_out, w_mem_o, w_out, ln1_g, ln1_b, w_peer_q, peer_keys, peer_u, peer_v, ln2_g, ln2_b):
    assert w_in.shape[0] == DEPTH == 1
    p = _prepare(w_in[0], sink[0], conv_w[0], w_mem_k[0], w_mem_v[0], w_attn_o[0], w_conv_out[0],
                 w_mem_o[0], w_out[0], ln1_g[0], ln1_b[0], w_peer_q[0], peer_keys[0], peer_u[0],
                 peer_v[0], ln2_g[0], ln2_b[0])
    return (_encode_group(x_prompt, mem_prompt, p), _encode_group(x_sample, mem_sample, p))
```

```python
import functools
import math

import jax
import jax.numpy as jnp
from jax import lax
from jax.experimental import pallas as pl
from jax.experimental.pallas import tpu as pltpu

D_MODEL = 2048
N_Q_HEADS = 8
N_KV_HEADS = 2
GQA_GROUP = N_Q_HEADS // N_KV_HEADS
HEAD_DIM = 128
WINDOW = 128
ATTN_WIDTH = N_Q_HEADS * HEAD_DIM
KV_WIDTH = N_KV_HEADS * HEAD_DIM
CONV_WIDTH = 1024
N_MEM = 256
MEM_HEADS = 4
MEM_HEAD_DIM = 256
MEM_WIDTH = MEM_HEADS * MEM_HEAD_DIM
PEER_HEADS = 8
N_KEYS = 128
N_EXPERTS = N_KEYS * N_KEYS
PEER_HALF = 128
PEER_TOPK = 16
LN_EPS = 1e-5
DEPTH = 1
ALPHA = (2.0 * DEPTH) ** 0.25
GATE_WIDTH = 3 * D_MODEL

COL_Q = 0
COL_K = COL_Q + ATTN_WIDTH
COL_V = COL_K + KV_WIDTH
COL_CH = COL_V + KV_WIDTH
COL_CB = COL_CH + CONV_WIDTH
COL_CC = COL_CB + CONV_WIDTH
PROJ_A_WIDTH = COL_CC + CONV_WIDTH
COL_MQ = 0
COL_GL = COL_MQ + MEM_WIDTH
PROJ_B_WIDTH = COL_GL + GATE_WIDTH

MXU_DT = jnp.bfloat16
GATE_DT = jnp.bfloat16
VMEM_LIMIT = 56 * 1024 * 1024

PROJ_TM = 1024
PROJ_A_TN = PROJ_A_WIDTH // 3
PROJ_B_TN = PROJ_B_WIDTH // 4
ATTN_BLOCKS = 8
MIX_T = 512
ROUTE_T = 256
ROUTE_LANES = 128
PEER_C = 512
PEER_EB = 256
PEER_SUB = 8
LN2_T = 512
TRANSPOSE_ROWS = 512

F32 = jnp.float32


def _cparams(*sem):
    return pltpu.CompilerParams(dimension_semantics=sem, vmem_limit_bytes=VMEM_LIMIT)


def _resident(shape):
    nd = len(shape)
    return pl.BlockSpec(shape, lambda *_: (0,) * nd, pipeline_mode=pl.Buffered(1))


def _layer_norm(x, g, b):
    mu = jnp.mean(x, axis=-1, keepdims=True)
    xc = x - mu
    var = jnp.mean(xc * xc, axis=-1, keepdims=True)
    return xc * lax.rsqrt(var + LN_EPS) * g + b


def _mm_kernel(x_ref, w_ref, o_ref, xb_ref):
    @pl.when(pl.program_id(1) == 0)
    def _():
        xb_ref[...] = x_ref[...].astype(xb_ref.dtype)

    o_ref[...] = jnp.dot(xb_ref[...], w_ref[...], preferred_element_type=F32).astype(o_ref.dtype)


def _matmul(x, w, out_dtype, tm, tn):
    m, k = x.shape
    n = w.shape[1]
    tm, tn = min(tm, m), min(tn, n)
    assert m % tm == 0 and n % tn == 0
    return pl.pallas_call(
        _mm_kernel,
        grid=(m // tm, n // tn),
        in_specs=[pl.BlockSpec((tm, k), lambda i, j: (i, 0)),
                  pl.BlockSpec((k, tn), lambda i, j: (0, j))],
        out_specs=pl.BlockSpec((tm, tn), lambda i, j: (i, j)),
        out_shape=jax.ShapeDtypeStruct((m, n), out_dtype),
        scratch_shapes=[pltpu.VMEM((tm, k), w.dtype)],
        compiler_params=_cparams("parallel", "arbitrary"),
        name="matmul",
    )(x, w)


def _attn_kernel(sink_ref, q_ref, kp_ref, km_ref, kn_ref, vp_ref, vm_ref, vn_ref, o_ref, *, tiles_per_seq):
    t = pl.program_id(0)
    pos = t % tiles_per_seq
    lo = jnp.where(pos == 0, WINDOW, 0)
    hi = jnp.where(pos == tiles_per_seq - 1, 2 * WINDOW, 3 * WINDOW)
    kall = jnp.concatenate([kp_ref[...], km_ref[...], kn_ref[...]], axis=0)
    vall = jnp.concatenate([vp_ref[...], vm_ref[...], vn_ref[...]], axis=0)
    qi = lax.broadcasted_iota(jnp.int32, (WINDOW, 3 * WINDOW), 0)
    si = lax.broadcasted_iota(jnp.int32, (WINDOW, 3 * WINDOW), 1)
    dist = jnp.abs(qi + WINDOW - si)
    band = dist <= WINDOW
    distf = dist.astype(F32)
    scale = 1.0 / math.sqrt(HEAD_DIM)
    n_blocks = q_ref.shape[0] // WINDOW
    for r in range(n_blocks):
        valid = band
        if r == 0:
            valid = valid & (si >= lo)
        if r == n_blocks - 1:
            valid = valid & (si < hi)
        rows = slice(r * WINDOW, (r + 1) * WINDOW)
        for kv in range(N_KV_HEADS):
            cols = slice(kv * HEAD_DIM, (kv + 1) * HEAD_DIM)
            kc = kall[r * WINDOW:(r + 3) * WINDOW, cols]
            vc = vall[r * WINDOW:(r + 3) * WINDOW, cols]
            heads = [kv * GQA_GROUP + g for g in range(GQA_GROUP)]
            q4 = jnp.concatenate([q_ref[rows, h * HEAD_DIM:(h + 1) * HEAD_DIM] for h in heads], axis=0)
            s4 = lax.dot_general(q4, kc, (((1,), (1,)), ((), ())), preferred_element_type=F32) * scale
            ps, inv = [], []
            for g, h in enumerate(heads):
                slope = 2.0 ** (-8.0 * (h + 1) / N_Q_HEADS)
                s = jnp.where(valid, s4[g * WINDOW:(g + 1) * WINDOW] - slope * distf, -jnp.inf)
                sk = sink_ref[h]
                m = jnp.maximum(jnp.max(s, axis=-1, keepdims=True), sk)
                p = jnp.exp(s - m)
                denom = jnp.sum(p, axis=-1, keepdims=True) + jnp.exp(sk - m)
                ps.append(p.astype(vc.dtype))
                inv.append(1.0 / denom)
            o4 = jnp.dot(jnp.concatenate(ps, axis=0), vc, preferred_element_type=F32)
            for g, h in enumerate(heads):
                o_ref[rows, h * HEAD_DIM:(h + 1) * HEAD_DIM] = (
                    o4[g * WINDOW:(g + 1) * WINDOW] * inv[g]).astype(o_ref.dtype)


def _attention(proj, sink, seq_len):
    n = proj.shape[0]
    ta = ATTN_BLOCKS * WINDOW
    assert seq_len % ta == 0 and n % seq_len == 0
    nblk = n // WINDOW
    kcol, vcol = COL_K // KV_WIDTH, COL_V // KV_WIDTH

    def halo(col, side):
        if side < 0:
            return pl.BlockSpec((WINDOW, KV_WIDTH), lambda t: (jnp.maximum(t * ATTN_BLOCKS - 1, 0), col))
        return pl.BlockSpec((WINDOW, KV_WIDTH), lambda t: (jnp.minimum((t + 1) * ATTN_BLOCKS, nblk - 1), col))

    def main(col):
        return pl.BlockSpec((ta, KV_WIDTH), lambda t: (t, col))

    return pl.pallas_call(
        functools.partial(_attn_kernel, tiles_per_seq=seq_len // ta),
        grid=(n // ta,),
        in_specs=[pl.BlockSpec(memory_space=pltpu.SMEM),
                  pl.BlockSpec((ta, ATTN_WIDTH), lambda t: (t, COL_Q // ATTN_WIDTH)),
                  halo(kcol, -1), main(kcol), halo(kcol, 1),
                  halo(vcol, -1), main(vcol), halo(vcol, 1)],
        out_specs=pl.BlockSpec((ta, ATTN_WIDTH), lambda t: (t, 0)),
        out_shape=jax.ShapeDtypeStruct((n, ATTN_WIDTH), proj.dtype),
        compiler_params=_cparams("parallel"),
        name="attention",
    )(sink, proj, proj, proj, proj, proj, proj, proj)


def _mixer_kernel(attn_ref, ch0, ch1, cb0, cb1, cc0, cc1, chp0, chp1, ccp0, ccp1, chn0, chn1, ccn0, ccn1,
                  g00, g01, g10, g11, g20, g21, mq_ref, mk_ref, mv_ref, convw_ref,
                  wa_ref, wc_ref, wm_ref, o_ref, *, tiles_per_seq):
    t = pl.program_id(0)
    pos = t % tiles_per_seq
    tt = ch0.shape[0]

    def wide(lo, hi, rows=slice(None)):
        return jnp.concatenate([lo[rows, :], hi[rows, :]], axis=-1).astype(F32)

    z = wide(cc0, cc1) * wide(ch0, ch1)
    zp = wide(ccp0, ccp1, slice(7, 8)) * wide(chp0, chp1, slice(7, 8))
    zn = wide(ccn0, ccn1, slice(0, 1)) * wide(chn0, chn1, slice(0, 1))
    zp = jnp.where(pos == 0, 0.0, zp)
    zn = jnp.where(pos == tiles_per_seq - 1, 0.0, zn)
    row = lax.broadcasted_iota(jnp.int32, z.shape, 0)
    z_up = jnp.where(row == 0, zp, pltpu.roll(z, 1, 0))
    z_dn = jnp.where(row == tt - 1, zn, pltpu.roll(z, tt - 1, 0))
    conv = z_up * convw_ref[0:1, :] + z * convw_ref[1:2, :] + z_dn * convw_ref[2:3, :]
    cv = (wide(cb0, cb1) * conv).astype(wc_ref.dtype)
    mscale = 1.0 / math.sqrt(MEM_HEAD_DIM)
    mo = []
    for hm in range(MEM_HEADS):
        cols = slice(hm * MEM_HEAD_DIM, (hm + 1) * MEM_HEAD_DIM)
        s = lax.dot_general(mq_ref[:, cols], mk_ref[:, cols], (((1,), (1,)), ((), ())),
                            preferred_element_type=F32) * mscale
        p = jnp.exp(s - jnp.max(s, axis=-1, keepdims=True))
        inv = 1.0 / jnp.sum(p, axis=-1, keepdims=True)
        o = jnp.dot(p.astype(mv_ref.dtype), mv_ref[:, cols], preferred_element_type=F32) * inv
        mo.append(o.astype(wm_ref.dtype))
    memo = jnp.concatenate(mo, axis=-1)
    a_out = jnp.dot(attn_ref[...], wa_ref[...], preferred_element_type=F32)
    c_out = jnp.dot(cv, wc_ref[...], preferred_element_type=F32)
    m_out = jnp.dot(memo, wm_ref[...], preferred_element_type=F32)
    merged = (jax.nn.sigmoid(wide(g00, g01)) * a_out
              + jax.nn.sigmoid(wide(g10, g11)) * c_out
              + jax.nn.sigmoid(wide(g20, g21)) * m_out)
    o_ref[...] = merged.astype(o_ref.dtype)


def _mixer(attn, proj, proj_b, mem_k, mem_v, conv_w, wa, wc, wm, seq_len):
    n = attn.shape[0]
    tt = MIX_T
    assert seq_len % tt == 0
    tiles_per_seq = seq_len // tt
    n8 = n // 8

    def halves(make, start, width):
        half = width // 2
        assert start % half == 0
        return [make(start // half + k, half) for k in range(2)]

    def col(block, width):
        return pl.BlockSpec((tt, width), lambda t: (t, block))

    def prev8(block, width):
        return pl.BlockSpec((8, width), lambda t: (jnp.maximum(t * (tt // 8) - 1, 0), block))

    def next8(block, width):
        return pl.BlockSpec((8, width), lambda t: (jnp.minimum((t + 1) * (tt // 8), n8 - 1), block))

    def mem_spec():
        return pl.BlockSpec((N_MEM, MEM_WIDTH), lambda t: (t // tiles_per_seq, 0))

    conv_specs = [s for start in (COL_CH, COL_CB, COL_CC) for s in halves(col, start, CONV_WIDTH)]
    halo_specs = [s for make, start in ((prev8, COL_CH), (prev8, COL_CC), (next8, COL_CH), (next8, COL_CC))
                  for s in halves(make, start, CONV_WIDTH)]
    gate_specs = [s for k in range(3) for s in halves(col, COL_GL + k * D_MODEL, D_MODEL)]
    assert COL_MQ % MEM_WIDTH == 0
    return pl.pallas_call(
        functools.partial(_mixer_kernel, tiles_per_seq=tiles_per_seq),
        grid=(n // tt,),
        in_specs=[pl.BlockSpec((tt, ATTN_WIDTH), lambda t: (t, 0)), *conv_specs, *halo_specs, *gate_specs,
                  col(COL_MQ // MEM_WIDTH, MEM_WIDTH), mem_spec(), mem_spec(),
                  _resident(conv_w.shape), _resident(wa.shape), _resident(wc.shape), _resident(wm.shape)],
        out_specs=pl.BlockSpec((tt, D_MODEL), lambda t: (t, 0)),
        out_shape=jax.ShapeDtypeStruct((n, D_MODEL), attn.dtype),
        compiler_params=_cparams("parallel"),
        name="mixer",
    )(attn, *[proj] * (len(conv_specs) + len(halo_specs)), *[proj_b] * (len(gate_specs) + 1),
      mem_k, mem_v, conv_w, wa, wc, wm)


def _sort_network(n):
    pairs, p = [], 1
    while p < n:
        k = p
        while k >= 1:
            for j in range(k % p, n - k, 2 * k):
                for i in range(min(k, n - j - k)):
                    if (i + j) // (2 * p) == (i + j + k) // (2 * p):
                        pairs.append((i + j, i + j + k))
            k //= 2
        p *= 2
    return pairs


def _compare_exchange(v, i, j):
    v[i], v[j] = jnp.maximum(v[i], v[j]), jnp.minimum(v[i], v[j])


def _network_sort(v, n_valid):
    v = list(v)
    for i, j in _sort_network(PEER_TOPK):
        if j < n_valid:
            _compare_exchange(v, i, j)
    return v


def _merge_step(v, n_valid, shift):
    assert 2 * n_valid >= PEER_TOPK
    other = [pltpu.roll(x, shift, 0) for x in v[:n_valid]]
    merged = []
    for a in range(PEER_TOPK):
        b = PEER_TOPK - 1 - a
        if a < n_valid and b < n_valid:
            merged.append(jnp.maximum(v[a], other[b]))
        else:
            merged.append(v[a] if a < n_valid else other[b])
    for d in (8, 4, 2, 1):
        for a in range(PEER_TOPK):
            if a & d == 0:
                _compare_exchange(merged, a, a + d)
    return merged


def _top16_rows_pair(va, vb, n_valid, sub):
    va = _merge_step(_network_sort(va, n_valid), n_valid, 4)
    vb = _merge_step(_network_sort(vb, n_valid), n_valid, 4)
    x = [jnp.where(sub < 4, a, b) for a, b in zip(va, vb)]
    x = _merge_step(x, PEER_TOPK, 2)
    x = _merge_step(x, PEER_TOPK, 1)
    spread = lambda v, k: jnp.broadcast_to(v[k:k + 1], v.shape)
    return [spread(v, 3) for v in x], [spread(v, 7) for v in x]


def _rows_to_block(rows, sub):
    out = rows[0]
    for k in range(1, 8):
        out = jnp.where(sub == k, rows[k], out)
    return out


def _all_sublanes(x, op):
    for shift in (4, 2, 1):
        x = op(x, pltpu.roll(x, shift, 0))
    return x


def _first_true(pred, thr):
    assert len(thr) == PEER_TOPK == 16
    q1 = pred(thr[7])
    q2 = pred(jnp.where(q1, thr[3], thr[11]))
    q3 = pred(jnp.where(q1, jnp.where(q2, thr[1], thr[5]), jnp.where(q2, thr[9], thr[13])))
    low = jnp.where(q2, jnp.where(q3, thr[0], thr[2]), jnp.where(q3, thr[4], thr[6]))
    high = jnp.where(q2, jnp.where(q3, thr[8], thr[10]), jnp.where(q3, thr[12], thr[14]))
    q4 = pred(jnp.where(q1, low, high))
    k = (jnp.where(q1, 0.0, 8.0) + jnp.where(q2, 0.0, 4.0)
         + jnp.where(q3, 0.0, 2.0) + jnp.where(q4, 0.0, 1.0))
    return jnp.where(pred(thr[15]), k, float(PEER_TOPK))


def _slabs(s):
    return [s[8 * v:8 * v + 8] for v in range(s.shape[0] // 8)]


def _pair_sum_candidates(t0, t1, sub):
    v0 = (_rows_to_block(t0[:8], sub), _rows_to_block(t0[8:], sub))
    v1_lo = _rows_to_block(t1[:8], sub)
    cand = [t0[0] + v1_lo, t0[0] + _rows_to_block(t1[8:], sub), t0[1] + v1_lo]
    for a in range(2, 8):
        cand.append(jnp.where(sub < PEER_TOPK // (a + 1), t0[a] + v1_lo, -jnp.inf))
    cand.append(v0[1] + t1[0])
    return cand, v0


def _route_finish(slabs0, slabs1, t0, t1, v0, thr):
    e0 = [jnp.exp(x - t0[0]) for x in v0]
    zacc = [jnp.zeros_like(thr), jnp.zeros_like(thr)]
    lowest = []
    for b in range(PEER_TOPK):
        e1b = jnp.exp(t1[b] - t1[0])
        low = None
        for half in range(2):
            sel = (v0[half] + t1[b]) >= thr
            zacc[half] = zacc[half] + jnp.where(sel, e1b, 0.0)
            cur = jnp.where(sel, v0[half], jnp.inf)
            low = cur if low is None else jnp.minimum(low, cur)
        lowest.append(_all_sublanes(low, jnp.minimum))
    inv_z = 1.0 / _all_sublanes(e0[0] * zacc[0] + e0[1] * zacc[1], jnp.add)
    c0, m0, r1, m1 = [], [], [], []
    for v in range(len(slabs0)):
        c0.append(_first_true(lambda thr: slabs0[v] < thr, lowest))
        r1.append(_first_true(lambda thr: slabs1[v] >= thr, t1))
        m0.append(jnp.exp(slabs0[v] - t0[0]) * inv_z)
        m1.append(jnp.exp(slabs1[v] - t1[0]))
    cat = lambda xs: jnp.concatenate(xs, axis=0)
    return cat(c0), cat(m0), cat(r1), cat(m1)


def _route_column_pair(scores_a, scores_b):
    (a0, a1), (b0, b1) = [tuple(_slabs(s) for s in col) for col in (scores_a, scores_b)]
    sub = lax.broadcasted_iota(jnp.int32, a0[0].shape, 0)
    n_slab = len(a0)
    ta0, tb0 = _top16_rows_pair(a0, b0, n_slab, sub)
    ta1, tb1 = _top16_rows_pair(a1, b1, n_slab, sub)
    cand_a, va0 = _pair_sum_candidates(ta0, ta1, sub)
    cand_b, vb0 = _pair_sum_candidates(tb0, tb1, sub)
    pad = [None] * (PEER_TOPK - len(cand_a))
    sums_a, sums_b = _top16_rows_pair(cand_a + pad, cand_b + pad, len(cand_a), sub)
    return (_route_finish(a0, a1, ta0, ta1, va0, sums_a[PEER_TOPK - 1]),
            _route_finish(b0, b1, tb0, tb1, vb0, sums_b[PEER_TOPK - 1]))


COUNT_MASK = 0xFF


def _pack_count_factor(count, factor):
    assert jnp.dtype(GATE_DT).itemsize == 2
    top = lax.bitcast_convert_type(factor.astype(GATE_DT).astype(F32), jnp.uint32)
    return top | count.astype(jnp.int32).astype(jnp.uint32)


def _ln1_route_kernel(x_ref, mg_ref, wo_ref, g_ref, b_ref, wq_ref, keys_ref,
                      h_ref, ht_ref, cm_ref, r1_ref, m1_ref, qt_scr):
    mix = jnp.dot(mg_ref[...], wo_ref[...], preferred_element_type=F32)
    h = _layer_norm(ALPHA * x_ref[...] + mix, g_ref[...], b_ref[...])
    h_ref[...] = h
    ht = h.T.astype(ht_ref.dtype)
    ht_ref[...] = ht
    qt_scr[...] = jnp.dot(wq_ref[...], ht, preferred_element_type=F32).astype(qt_scr.dtype)

    def scores(hd, lanes):
        base = pl.multiple_of(hd * (2 * PEER_HALF), 2 * PEER_HALF)
        return (jnp.dot(keys_ref[2 * hd], qt_scr[pl.ds(base, PEER_HALF), lanes], preferred_element_type=F32),
                jnp.dot(keys_ref[2 * hd + 1], qt_scr[pl.ds(base + PEER_HALF, PEER_HALF), lanes],
                        preferred_element_type=F32))

    def head(hd, carry):
        out_rows = pl.ds(pl.multiple_of(hd * N_KEYS, N_KEYS), N_KEYS)
        for pair in range(x_ref.shape[0] // (2 * ROUTE_LANES)):
            cols = [slice((2 * pair + k) * ROUTE_LANES, (2 * pair + k + 1) * ROUTE_LANES) for k in range(2)]
            routed = _route_column_pair(scores(hd, cols[0]), scores(hd, cols[1]))
            for lanes, (c0, m0, r1, m1) in zip(cols, routed):
                cm_ref[out_rows, lanes] = _pack_count_factor(c0, m0)
                r1_ref[out_rows, lanes] = r1.astype(r1_ref.dtype)
                m1_ref[out_rows, lanes] = m1.astype(m1_ref.dtype)
        return carry

    lax.fori_loop(0, PEER_HEADS, head, 0)


def _ln1_route(x, merged, w_out, ln_g, ln_b, wq_t, keys):
    n = x.shape[0]
    tt = ROUTE_T
    rows = PEER_HEADS * N_KEYS
    tok = lambda t: (t, 0)
    tok_t = lambda t: (0, t)
    return pl.pallas_call(
        _ln1_route_kernel,
        grid=(n // tt,),
        in_specs=[pl.BlockSpec((tt, D_MODEL), tok), pl.BlockSpec((tt, D_MODEL), tok),
                  _resident(w_out.shape), _resident(ln_g.shape), _resident(ln_b.shape),
                  _resident(wq_t.shape), _resident(keys.shape)],
        out_specs=[pl.BlockSpec((tt, D_MODEL), tok), pl.BlockSpec((D_MODEL, tt), tok_t),
                   pl.BlockSpec((rows, tt), tok_t), pl.BlockSpec((rows, tt), tok_t),
                   pl.BlockSpec((rows, tt), tok_t)],
        out_shape=[jax.ShapeDtypeStruct((n, D_MODEL), F32),
                   jax.ShapeDtypeStruct((D_MODEL, n), MXU_DT),
                   jax.ShapeDtypeStruct((rows, n), jnp.uint32),
                   jax.ShapeDtypeStruct((rows, n), GATE_DT), jax.ShapeDtypeStruct((rows, n), GATE_DT)],
        scratch_shapes=[pltpu.VMEM((PEER_HEADS * 2 * PEER_HALF, tt), MXU_DT)],
        compiler_params=_cparams("parallel"),
        name="ln1_route",
    )(x, merged, w_out, ln_g, ln_b, wq_t, keys)


def _peer_gate_block(block, pre, cm_ref, r1_ref, m1_ref):
    act = (0.5 * pre * (1.0 + lax.erf(pre * math.sqrt(0.5)))).astype(GATE_DT)
    per_block = pre.shape[0] // N_KEYS
    lanes = pre.shape[1]
    tiles = (N_KEYS // 16, 16, lanes)
    out = []
    for ii in range(per_block):
        i = block * per_block + ii
        gate = None
        for hd in range(PEER_HEADS):
            word = jnp.broadcast_to(cm_ref[pl.ds(hd * N_KEYS + i, 1), :], (16, lanes))
            m0 = lax.bitcast_convert_type(word, F32).astype(GATE_DT)
            c0 = (word & COUNT_MASK).astype(jnp.int32).astype(F32).astype(GATE_DT)
            r1 = r1_ref[hd * N_KEYS:(hd + 1) * N_KEYS, :].reshape(tiles)
            m1 = m1_ref[hd * N_KEYS:(hd + 1) * N_KEYS, :].reshape(tiles)
            term = jnp.where(r1 < c0[None], m1, jnp.zeros_like(m1)) * m0[None]
            gate = term if gate is None else gate + term
        gate = gate.reshape(N_KEYS, lanes)
        out.append((gate * act[ii * N_KEYS:(ii + 1) * N_KEYS, :]).astype(MXU_DT))
    return jnp.concatenate(out, axis=0)


def _peer_kernel(ht_ref, u_ref, vt_ref, cm_ref, r1_ref, m1_ref, o_ref):
    g = pl.program_id(1)
    eb = PEER_EB
    n_sub = u_ref.shape[0] // eb

    @pl.when(g == 0)
    def _():
        o_ref[...] = jnp.zeros_like(o_ref)

    routing = (cm_ref, r1_ref, m1_ref)
    pre = [jnp.dot(u_ref[k * eb:(k + 1) * eb, :], ht_ref[...], preferred_element_type=F32)
           for k in range(n_sub)]
    for k in range(n_sub):
        w = _peer_gate_block(g * n_sub + k, pre[k], *routing)
        o_ref[...] += jnp.dot(vt_ref[:, k * eb:(k + 1) * eb], w, preferred_element_type=F32)


def _peer_dense(ht, u, vt, cm, r1, m1):
    n = ht.shape[1]
    c, step = PEER_C, PEER_EB * PEER_SUB
    rows = PEER_HEADS * N_KEYS
    tok_t = lambda t, g: (0, t)
    tile = lambda r: pl.BlockSpec((r, c), tok_t)
    return pl.pallas_call(
        _peer_kernel,
        grid=(n // c, N_EXPERTS // step),
        in_specs=[tile(D_MODEL),
                  pl.BlockSpec((step, D_MODEL), lambda t, g: (g, 0)),
                  pl.BlockSpec((D_MODEL, step), lambda t, g: (0, g)),
                  tile(rows), tile(rows), tile(rows)],
        out_specs=tile(D_MODEL),
        out_shape=jax.ShapeDtypeStruct((D_MODEL, n), F32),
        compiler_params=_cparams("parallel", "arbitrary"),
        name="peer_dense",
    )(ht, u, vt, cm, r1, m1)


def _transpose_cast_kernel(x_ref, o_ref):
    o_ref[...] = x_ref[...].T.astype(o_ref.dtype)


def _transpose_cast(table, dtype):
    rows, cols = table.shape
    tr = TRANSPOSE_ROWS
    assert rows % tr == 0
    return pl.pallas_call(
        _transpose_cast_kernel,
        grid=(rows // tr,),
        in_specs=[pl.BlockSpec((tr, cols), lambda i: (i, 0))],
        out_specs=pl.BlockSpec((cols, tr), lambda i: (0, i)),
        out_shape=jax.ShapeDtypeStruct((cols, rows), dtype),
        compiler_params=_cparams("parallel"),
        name="transpose_cast",
    )(table)


def _ln2_kernel(h_ref, pt_ref, g_ref, b_ref, o_ref):
    o_ref[...] = _layer_norm(ALPHA * h_ref[...] + pt_ref[...].T, g_ref[...], b_ref[...])


def _ln2(h, peer_t, ln_g, ln_b):
    n = h.shape[0]
    tt = LN2_T
    return pl.pallas_call(
        _ln2_kernel,
        grid=(n // tt,),
        in_specs=[pl.BlockSpec((tt, D_MODEL), lambda t: (t, 0)),
                  pl.BlockSpec((D_MODEL, tt), lambda t: (0, t)),
                  _resident(ln_g.shape), _resident(ln_b.shape)],
        out_specs=pl.BlockSpec((tt, D_MODEL), lambda t: (t, 0)),
        out_shape=jax.ShapeDtypeStruct((n, D_MODEL), F32),
        compiler_params=_cparams("parallel"),
        name="ln2",
    )(h, peer_t, ln_g, ln_b)


def _prepare(w_in, sink, conv_w, w_mem_k, w_mem_v, w_attn_o, w_conv_out, w_mem_o, w_out,
             ln1_g, ln1_b, w_peer_q, peer_keys, peer_u, peer_v, ln2_g, ln2_b):
    assert w_in.shape[1] == PROJ_A_WIDTH + PROJ_B_WIDTH
    row = lambda v: v.reshape(1, -1).astype(F32)
    return dict(
        w_a=w_in[:, :PROJ_A_WIDTH].astype(MXU_DT), w_b=w_in[:, PROJ_A_WIDTH:].astype(MXU_DT),
        sink=sink.astype(F32), conv_w=conv_w.astype(F32),
        w_mem_k=w_mem_k.astype(MXU_DT), w_mem_v=w_mem_v.astype(MXU_DT),
        wa=w_attn_o.astype(MXU_DT), wc=w_conv_out.astype(MXU_DT), wm=w_mem_o.astype(MXU_DT),
        w_out=w_out.astype(MXU_DT), ln1_g=row(ln1_g), ln1_b=row(ln1_b),
        wq_t=w_peer_q.astype(MXU_DT).T,
        keys=peer_keys.reshape(PEER_HEADS * 2, N_KEYS, PEER_HALF).astype(MXU_DT),
        u=peer_u.astype(MXU_DT), vt=_transpose_cast(peer_v, MXU_DT), ln2_g=row(ln2_g), ln2_b=row(ln2_b))


def _encode_group(x, mem, p):
    b, s, d = x.shape
    x2 = x.reshape(b * s, d)
    mem2 = mem.reshape(b * N_MEM, d)
    proj = _matmul(x2, p["w_a"], MXU_DT, PROJ_TM, PROJ_A_TN)
    proj_b = _matmul(x2, p["w_b"], MXU_DT, PROJ_TM, PROJ_B_TN)
    mem_k = _matmul(mem2, p["w_mem_k"], MXU_DT, 512, 512)
    mem_v = _matmul(mem2, p["w_mem_v"], MXU_DT, 512, 512)
    attn = _attention(proj, p["sink"], s)
    merged = _mixer(attn, proj, proj_b, mem_k, mem_v, p["conv_w"], p["wa"], p["wc"], p["wm"], s)
    h, ht, cm, r1, m1 = _ln1_route(x2, merged, p["w_out"], p["ln1_g"], p["ln1_b"], p["wq_t"], p["keys"])
    peer_t = _peer_dense(ht, p["u"], p["vt"], cm, r1, m1)
    return _ln2(h, peer_t, p["ln2_g"], p["ln2_b"]).reshape(b, s, d)


def kernel(x_prompt, x_sample, mem_prompt, mem_sample, w_in, sink, conv_w, w_mem_k, w_mem_v, w_attn_o,
           w_conv_out, w_mem_o, w_out, ln1_g, ln1_b, w_peer_q, peer_keys, peer_u, peer_v, ln2_g, ln2_b):
    assert w_in.shape[0] == DEPTH == 1
    p = _prepare(w_in[0], sink[0], conv_w[0], w_mem_k[0], w_mem_v[0], w_attn_o[0], w_conv_out[0],
                 w_mem_o[0], w_out[0], ln1_g[0], ln1_b[0], w_peer_q[0], peer_keys[0], peer_u[0],
                 peer_v[0], ln2_g[0], ln2_b[0])
    return (_encode_group(x_prompt, mem_prompt, p), _encode_group(x_sample, mem_sample, p))
```
